```python
import math
import jax, jax.numpy as jnp
from jax import lax
import numpy as np

D_MODEL = 2048
BATCH = 4
SEQ = 4096
DEPTH = 2

N_A = DEPTH // 2
N_B = DEPTH - N_A
CONV_WIDTH = 31
D_FF = 5632
FFN_CONV_WIDTH = 3
HEAD_DIM = 128
N_HEADS = D_MODEL // HEAD_DIM
BLOCK = 256
TOP_K = 3
Q_CHUNK = 16
EPS = 1e-6
NEG_INF = -1e30

kernel_name = "yoco_conformer_moba_hybrid"


def rms_norm(x, g):
    xf = x.astype(jnp.float32)
    y = xf * lax.rsqrt(jnp.mean(xf * xf, axis=-1, keepdims=True) + EPS)
    return (y * g.astype(jnp.float32)).astype(x.dtype)


def layer_norm(x, g, b):
    xf = x.astype(jnp.float32)
    mu = jnp.mean(xf, axis=-1, keepdims=True)
    var = jnp.mean(jnp.square(xf - mu), axis=-1, keepdims=True)
    y = (xf - mu) * lax.rsqrt(var + EPS)
    return (y * g.astype(jnp.float32) + b.astype(jnp.float32)).astype(x.dtype)


def causal_dwconv(x, w):
    width = w.shape[0]
    return lax.conv_general_dilated(
        x, w[:, None, :].astype(x.dtype), window_strides=(1,),
        padding=[(width - 1, 0)], dimension_numbers=("NWC", "WIO", "NWC"),
        feature_group_count=x.shape[-1])


def alibi_slopes():
    h = jnp.arange(1, N_HEADS + 1, dtype=jnp.float32)
    return jnp.exp2(-8.0 * h / N_HEADS)


def conformer_conv(h, w_pw1, b_pw1, w_dw, b_dw, ln_g, ln_b, w_pw2, b_pw2):
    u = h @ w_pw1 + b_pw1
    a, gate = jnp.split(u, 2, axis=-1)
    u = a * jax.nn.sigmoid(gate)
    u = causal_dwconv(u, w_dw) + b_dw
    u = jax.nn.silu(layer_norm(u, ln_g, ln_b))
    return u @ w_pw2 + b_pw2


def conv_ffn(h, w_in, w_conv, w_out):
    g, u = jnp.split(h @ w_in, 2, axis=-1)
    g = causal_dwconv(g, w_conv)
    return (jax.nn.silu(g) * u) @ w_out


def to_heads(t):
    b, s, _ = t.shape
    return t.reshape(b, s, N_HEADS, HEAD_DIM).transpose(0, 2, 1, 3)


def shared_kv(x, kv_norm_g, w_kv, k_norm_g):
    b, s, _ = x.shape
    k, v = jnp.split(rms_norm(x, kv_norm_g) @ w_kv, 2, axis=-1)
    k = rms_norm(to_heads(k), k_norm_g)
    v = to_heads(v)
    nb = -(-s // BLOCK)
    pad = nb * BLOCK - s
    k = jnp.pad(k, ((0, 0), (0, 0), (0, pad), (0, 0)))
    v = jnp.pad(v, ((0, 0), (0, 0), (0, pad), (0, 0)))
    kb = k.reshape(b, N_HEADS, nb, BLOCK, HEAD_DIM)
    vb = v.reshape(b, N_HEADS, nb, BLOCK, HEAD_DIM)
    k_mean = jnp.mean(kb.astype(jnp.float32), axis=3)
    return kb, vb, k_mean


def moba_attention(q, kb, vb, k_mean):
    b, h, s, dh = q.shape
    nb = kb.shape[2]
    k_sel = min(TOP_K, nb)
    n_chunks = s // Q_CHUNK
    scale = dh ** -0.5
    slopes = alibi_slopes()
    b_idx = jnp.arange(b)[:, None, None, None]
    h_idx = jnp.arange(h)[None, :, None, None]
    q_chunks = q.reshape(b, h, n_chunks, Q_CHUNK, dh).transpose(2, 0, 1, 3, 4)

    def one_chunk(args):
        qc, c = args
        t = c * Q_CHUNK + jnp.arange(Q_CHUNK)
        cur = (c * Q_CHUNK) // BLOCK
        qf = qc.astype(jnp.float32)
        gate = jnp.einsum("bhqd,bhnd->bhqn", qf, k_mean)
        past = jnp.arange(nb) < cur
        gate = jnp.where(past, gate, NEG_INF)
        _, idx = lax.top_k(gate, k_sel)
        valid = idx < cur
        ks = kb[b_idx, h_idx, idx]
        vs = vb[b_idx, h_idx, idx]
        s_sel = jnp.einsum("bhqd,bhqkjd->bhqkj", qc, ks).astype(jnp.float32) * scale
        pos_sel = (idx[..., None] * BLOCK + jnp.arange(BLOCK)).astype(jnp.float32)
        dist_sel = t.astype(jnp.float32)[None, None, :, None, None] - pos_sel
        s_sel = s_sel - slopes[None, :, None, None, None] * dist_sel
        s_sel = jnp.where(valid[..., None], s_sel, NEG_INF).reshape(b, h, Q_CHUNK, k_sel * BLOCK)
        k_own = lax.dynamic_index_in_dim(kb, cur, axis=2, keepdims=False)
        v_own = lax.dynamic_index_in_dim(vb, cur, axis=2, keepdims=False)
        s_own = jnp.einsum("bhqd,bhjd->bhqj", qc, k_own).astype(jnp.float32) * scale
        dist_own = (t[:, None] - (cur * BLOCK + jnp.arange(BLOCK))[None, :]).astype(jnp.float32)
        s_own = jnp.where(dist_own >= 0, s_own - slopes[None, :, None, None] * dist_own, NEG_INF)
        p = jax.nn.softmax(jnp.concatenate([s_sel, s_own], axis=-1), axis=-1)
        p_sel = p[..., : k_sel * BLOCK].reshape(b, h, Q_CHUNK, k_sel, BLOCK).astype(vs.dtype)
        p_own = p[..., k_sel * BLOCK:].astype(v_own.dtype)
        return (jnp.einsum("bhqkj,bhqkjd->bhqd", p_sel, vs)
                + jnp.einsum("bhqj,bhjd->bhqd", p_own, v_own))

    out = lax.map(one_chunk, (q_chunks, jnp.arange(n_chunks)))
    return out.transpose(1, 2, 0, 3, 4).reshape(b, h, s, dh)


def setup_inputs(seed: int = 0) -> dict:
    key = jax.random.key(seed)
    ks = jax.random.split(key, 24)
    f32 = jnp.float32
    D, F, dh = D_MODEL, D_FF, HEAD_DIM

    def w(k, shape, fan_in):
        return jax.random.normal(k, shape, f32) * fan_in ** -0.5

    def gain(k, shape):
        return 1.0 + 0.02 * jax.random.normal(k, shape, f32)

    def bias(k, shape):
        return 0.02 * jax.random.normal(k, shape, f32)

    return {
        "x": jax.random.normal(ks[0], (BATCH, SEQ, D), f32),
        "conv_norm_g": gain(ks[1], (N_A, D)),
        "conv_w_pw1": w(ks[2], (N_A, D, 2 * D), D),
        "conv_b_pw1": bias(ks[3], (N_A, 2 * D)),
        "conv_w_dw": w(ks[4], (N_A, CONV_WIDTH, D), CONV_WIDTH),
        "conv_b_dw": bias(ks[5], (N_A, D)),
        "conv_ln_g": gain(ks[6], (N_A, D)),
        "conv_ln_b": bias(ks[7], (N_A, D)),
        "conv_w_pw2": w(ks[8], (N_A, D, D), D),
        "conv_b_pw2": bias(ks[9], (N_A, D)),
        "kv_norm_g": gain(ks[10], (D,)),
        "w_kv": w(ks[11], (D, 2 * D), D),
        "k_norm_g": gain(ks[12], (dh,)),
        "attn_norm_g": gain(ks[13], (N_B, D)),
        "w_q": w(ks[14], (N_B, D, D), D),
        "q_norm_g": gain(ks[15], (N_B, dh)),
        "w_o": w(ks[16], (N_B, D, D), D),
        "ffn_norm_g": gain(ks[17], (DEPTH, D)),
        "ffn_w_in": w(ks[18], (DEPTH, D, 2 * F), D),
        "ffn_w_conv": w(ks[19], (DEPTH, FFN_CONV_WIDTH, F), FFN_CONV_WIDTH),
        "ffn_w_out": w(ks[20], (DEPTH, F, D), F),
    }


def reference(x, conv_norm_g, conv_w_pw1, conv_b_pw1, conv_w_dw, conv_b_dw, conv_ln_g, conv_ln_b,
              conv_w_pw2, conv_b_pw2, kv_norm_g, w_kv, k_norm_g, attn_norm_g, w_q, q_norm_g, w_o,
              ffn_norm_g, ffn_w_in, ffn_w_conv, ffn_w_out):
    b, s, d = x.shape
    kv = None
    for layer in range(DEPTH):
        if layer < N_A:
            i = layer
            x = x + conformer_conv(rms_norm(x, conv_norm_g[i]), conv_w_pw1[i], conv_b_pw1[i],
                                   conv_w_dw[i], conv_b_dw[i], conv_ln_g[i], conv_ln_b[i],
                                   conv_w_pw2[i], conv_b_pw2[i])
        else:
            if kv is None:
                kv = shared_kv(x, kv_norm_g, w_kv, k_norm_g)
            kb, vb, k_mean = kv
            j = layer - N_A
            q = rms_norm(to_heads(rms_norm(x, attn_norm_g[j]) @ w_q[j]), q_norm_g[j])
            o = moba_attention(q, kb, vb, k_mean)
            x = x + o.transpose(0, 2, 1, 3).reshape(b, s, d) @ w_o[j]
        x = x + conv_ffn(rms_norm(x, ffn_norm_g[layer]), ffn_w_in[layer], ffn_w_conv[layer],
                         ffn_w_out[layer])
    return x
```

```python
import functools

import jax
import jax.numpy as jnp
from jax import lax
from jax.experimental import pallas as pl
from jax.experimental.pallas import tpu as pltpu

EPS = 1e-6
NEG_INF = -1e30
HEAD_DIM = 128
MOBA_BLOCK = 256
TOP_K = 3
LANES = 128
SUBLANES = 8
VMEM_LIMIT_BYTES = 56 * 1024 * 1024

F32 = jnp.float32
BF16 = jnp.bfloat16


def _params(n_axes):
    return pltpu.CompilerParams(dimension_semantics=("arbitrary",) * n_axes,
                                vmem_limit_bytes=VMEM_LIMIT_BYTES)


def _rms_normalize(x):
    return x * lax.rsqrt(jnp.mean(x * x, axis=-1, keepdims=True) + EPS)


def _silu(x):
    return x * jax.nn.sigmoid(x)


def _pw1_glu_kernel(x_ref, g_ref, wa_ref, wg_ref, ba_ref, bg_ref, u_ref, h_scr):
    @pl.when(pl.program_id(1) == 0)
    def _():
        h_scr[...] = (_rms_normalize(x_ref[...]) * g_ref[...]).astype(BF16)

    h = h_scr[...]
    a = jnp.dot(h, wa_ref[...], preferred_element_type=F32) + ba_ref[...]
    gate = jnp.dot(h, wg_ref[...], preferred_element_type=F32) + bg_ref[...]
    u_ref[...] = a * jax.nn.sigmoid(gate)


def _pw1_glu(x, gain, w1, b1, *, tm, tn):
    m, d = x.shape
    nj = d // tn
    return pl.pallas_call(
        _pw1_glu_kernel,
        grid=(m // tm, nj),
        in_specs=[
            pl.BlockSpec((tm, d), lambda i, j: (i, 0)),
            pl.BlockSpec((1, d), lambda i, j: (0, 0)),
            pl.BlockSpec((d, tn), lambda i, j: (0, j)),
            pl.BlockSpec((d, tn), lambda i, j: (0, nj + j)),
            pl.BlockSpec((1, tn), lambda i, j: (0, j)),
            pl.BlockSpec((1, tn), lambda i, j: (0, nj + j)),
        ],
        out_specs=pl.BlockSpec((tm, tn), lambda i, j: (i, j)),
        out_shape=jax.ShapeDtypeStruct((m, d), F32),
        scratch_shapes=[pltpu.VMEM((tm, d), BF16)],
        compiler_params=_params(2),
        name="pw1_glu",
    )(x, gain, w1, w1, b1, b1)


CONV_HALO = 32
CONV_ROWS = 64


def _conv_ln_pw2_kernel(u_ref, uh_ref, x_ref, wdw_ref, bdw_ref, lng_ref, lnb_ref, w2_ref, b2_ref,
                        o_ref, ub_scr, y_scr, *, tm, seq, width):
    i = pl.program_id(0)
    d = u_ref.shape[1]
    first = (i * tm) % seq == 0

    @pl.when(first)
    def _():
        ub_scr[0:CONV_HALO, :] = jnp.zeros((CONV_HALO, d), F32)

    @pl.when(jnp.logical_not(first))
    def _():
        ub_scr[0:CONV_HALO, :] = uh_ref[...]

    ub_scr[CONV_HALO:CONV_HALO + tm, :] = u_ref[...]

    off0 = CONV_HALO - (width - 1)
    max_off = off0 + width - 1
    n_a = max_off // SUBLANES + 1
    win_rows = CONV_ROWS + (n_a - 1) * SUBLANES

    def strip(c, carry):
        col = pl.multiple_of(c * LANES, LANES)
        for r in range(tm // CONV_ROWS):
            base = r * CONV_ROWS
            acc = jnp.zeros((CONV_ROWS, LANES), F32)
            for b in range(SUBLANES):
                rows_needed = win_rows if b == 0 else win_rows - SUBLANES
                win = ub_scr[pl.ds(base + b, rows_needed), pl.ds(col, LANES)]
                for a in range(n_a):
                    off = SUBLANES * a + b
                    k = off - off0
                    if 0 <= k < width and SUBLANES * a + CONV_ROWS <= rows_needed:
                        wk = wdw_ref[pl.ds(k, 1), pl.ds(col, LANES)]
                        acc = acc + wk * win[SUBLANES * a:SUBLANES * a + CONV_ROWS, :]
            y_scr[pl.ds(base, CONV_ROWS), pl.ds(col, LANES)] = acc
        return carry

    lax.fori_loop(0, d // LANES, strip, 0)

    y = y_scr[...] + bdw_ref[...]
    mu = jnp.mean(y, axis=-1, keepdims=True)
    yc = y - mu
    var = jnp.mean(yc * yc, axis=-1, keepdims=True)
    yn = yc * lax.rsqrt(var + EPS) * lng_ref[...] + lnb_ref[...]
    act = _silu(yn).astype(BF16)
    o_ref[...] = x_ref[...] + (jnp.dot(act, w2_ref[...], preferred_element_type=F32) + b2_ref[...])


def _conv_ln_pw2(u, x, w_dw, b_dw, ln_g, ln_b, w2, b2, *, tm, seq):
    m, d = u.shape
    width = w_dw.shape[0]
    assert width - 1 <= CONV_HALO and tm % CONV_ROWS == 0 and tm % CONV_HALO == 0
    hb = tm // CONV_HALO
    kern = functools.partial(_conv_ln_pw2_kernel, tm=tm, seq=seq, width=width)
    row = pl.BlockSpec((1, d), lambda i: (0, 0))
    return pl.pallas_call(
        kern,
        grid=(m // tm,),
        in_specs=[
            pl.BlockSpec((tm, d), lambda i: (i, 0)),
            pl.BlockSpec((CONV_HALO, d), lambda i: (jnp.maximum(i * hb - 1, 0), 0)),
            pl.BlockSpec((tm, d), lambda i: (i, 0)),
            pl.BlockSpec((width, d), lambda i: (0, 0)),
            row, row, row,
            pl.BlockSpec((d, d), lambda i: (0, 0)),
            row,
        ],
        out_specs=pl.BlockSpec((tm, d), lambda i: (i, 0)),
        out_shape=jax.ShapeDtypeStruct((m, d), F32),
        scratch_shapes=[pltpu.VMEM((CONV_HALO + tm, d), F32), pltpu.VMEM((tm, d), F32)],
        compiler_params=_params(1),
        name="conv_ln_pw2",
    )(u, u, x, w_dw, b_dw, ln_g, ln_b, w2, b2)


def _conv_ffn_kernel(x_ref, g_ref, wg_ref, wu_ref, wc_ref, wo_ref, o_ref,
                     h_scr, acc_scr, gs_scr, tail_scr, *, tm, seq):
    i = pl.program_id(0)
    j = pl.program_id(1)
    nj = pl.num_programs(1)
    tf = wg_ref.shape[1]
    first = (i * tm) % seq == 0

    @pl.when(j == 0)
    def _():
        h_scr[...] = (_rms_normalize(x_ref[...]) * g_ref[...]).astype(BF16)

    h = h_scr[...]
    g = jnp.dot(h, wg_ref[...], preferred_element_type=F32)
    u = jnp.dot(h, wu_ref[...], preferred_element_type=F32)

    @pl.when(first)
    def _():
        gs_scr[0:SUBLANES, :] = jnp.zeros((SUBLANES, tf), F32)

    @pl.when(jnp.logical_not(first))
    def _():
        gs_scr[0:SUBLANES, :] = tail_scr[j]

    gs_scr[SUBLANES:SUBLANES + tm, :] = g
    tail_scr[j] = g[tm - SUBLANES:tm, :]
    g1 = gs_scr[pl.ds(SUBLANES - 1, tm), :]
    g2 = gs_scr[pl.ds(SUBLANES - 2, tm), :]
    gc = wc_ref[2:3, :] * g + wc_ref[1:2, :] * g1 + wc_ref[0:1, :] * g2
    act = (_silu(gc) * u).astype(BF16)
    part = jnp.dot(act, wo_ref[...], preferred_element_type=F32)

    @pl.when(j == 0)
    def _():
        acc_scr[...] = part

    @pl.when(j > 0)
    def _():
        acc_scr[...] += part

    @pl.when(j == nj - 1)
    def _():
        o_ref[...] = x_ref[...] + acc_scr[...]


def _conv_ffn(x, gain, w_in, w_conv, w_out, *, tm, tf, seq):
    m, d = x.shape
    f = w_out.shape[0]
    assert w_conv.shape[0] == 3 and f % tf == 0
    nj = f // tf
    kern = functools.partial(_conv_ffn_kernel, tm=tm, seq=seq)
    return pl.pallas_call(
        kern,
        grid=(m // tm, nj),
        in_specs=[
            pl.BlockSpec((tm, d), lambda i, j: (i, 0)),
            pl.BlockSpec((1, d), lambda i, j: (0, 0)),
            pl.BlockSpec((d, tf), lambda i, j: (0, j)),
            pl.BlockSpec((d, tf), lambda i, j: (0, nj + j)),
            pl.BlockSpec((3, tf), lambda i, j: (0, j)),
            pl.BlockSpec((tf, d), lambda i, j: (j, 0)),
        ],
        out_specs=pl.BlockSpec((tm, d), lambda i, j: (i, 0)),
        out_shape=jax.ShapeDtypeStruct((m, d), F32),
        scratch_shapes=[
            pltpu.VMEM((tm, d), BF16),
            pltpu.VMEM((tm, d), F32),
            pltpu.VMEM((SUBLANES + tm, tf), F32),
            pltpu.VMEM((nj, SUBLANES, tf), F32),
        ],
        compiler_params=_params(2),
        name="conv_ffn",
    )(x, gain, w_in, w_in, w_conv, w_out)


def _qkv_kernel(x_ref, gkv_ref, gq_ref, wk_ref, wv_ref, wq_ref, kng_ref, qng_ref,
                q_ref, ka_ref, v_ref, km_ref, hkv_scr, hq_scr, *, tm, seq, hps):
    i = pl.program_id(0)

    @pl.when(pl.program_id(1) == 0)
    def _():
        xn = _rms_normalize(x_ref[...])
        hkv_scr[...] = (xn * gkv_ref[...]).astype(BF16)
        hq_scr[...] = (xn * gq_ref[...]).astype(BF16)

    hkv = hkv_scr[...]
    kk = jnp.dot(hkv, wk_ref[...], preferred_element_type=F32)
    vv = jnp.dot(hkv, wv_ref[...], preferred_element_type=F32)
    qq = jnp.dot(hq_scr[...], wq_ref[...], preferred_element_type=F32)

    nblk = tm // MOBA_BLOCK
    blk0 = ((i * tm) % seq) // MOBA_BLOCK
    shift = MOBA_BLOCK.bit_length() - 1
    row_blk = blk0 + lax.shift_right_logical(
        lax.broadcasted_iota(jnp.int32, (tm, LANES), 0), jnp.full((tm, LANES), shift, jnp.int32))
    onehot = (lax.broadcasted_iota(jnp.int32, (tm, LANES), 1) == row_blk).astype(BF16)

    for hh in range(hps):
        sl = slice(hh * HEAD_DIM, (hh + 1) * HEAD_DIM)
        kn = _rms_normalize(kk[:, sl]) * kng_ref[...]
        q_ref[:, sl] = _rms_normalize(qq[:, sl]) * qng_ref[...]
        ka_ref[0, hh, :, 0:HEAD_DIM] = kn.astype(BF16)
        ka_ref[0, hh, :, HEAD_DIM:HEAD_DIM + LANES] = onehot
        v_ref[0, hh] = vv[:, sl].astype(BF16)
        for bb in range(nblk):
            km_ref[0, 0, hh, bb:bb + 1, :] = jnp.mean(
                kn[bb * MOBA_BLOCK:(bb + 1) * MOBA_BLOCK, :], axis=0, keepdims=True)


def _qkv(x, g_kv, g_q, w_kv, w_q, k_norm_g, q_norm_g, *, tm, tn, batch, seq):
    m, d = x.shape
    assert tn % HEAD_DIM == 0 and tm % MOBA_BLOCK == 0 and seq % tm == 0
    nj = d // tn
    hps = tn // HEAD_DIM
    n_heads = d // HEAD_DIM
    tps = seq // tm
    nblk = tm // MOBA_BLOCK
    kern = functools.partial(_qkv_kernel, tm=tm, seq=seq, hps=hps)
    row_d = pl.BlockSpec((1, d), lambda i, j: (0, 0))
    row_h = pl.BlockSpec((1, HEAD_DIM), lambda i, j: (0, 0))
    return pl.pallas_call(
        kern,
        grid=(m // tm, nj),
        in_specs=[
            pl.BlockSpec((tm, d), lambda i, j: (i, 0)),
            row_d, row_d,
            pl.BlockSpec((d, tn), lambda i, j: (0, j)),
            pl.BlockSpec((d, tn), lambda i, j: (0, nj + j)),
            pl.BlockSpec((d, tn), lambda i, j: (0, j)),
            row_h, row_h,
        ],
        out_specs=[
            pl.BlockSpec((tm, tn), lambda i, j: (i, j)),
            pl.BlockSpec((1, hps, tm, HEAD_DIM + LANES), lambda i, j: (i // tps, j, i % tps, 0)),
            pl.BlockSpec((1, hps, tm, HEAD_DIM), lambda i, j: (i // tps, j, i % tps, 0)),
            pl.BlockSpec((1, 1, hps, nblk, HEAD_DIM), lambda i, j: (i // tps, i % tps, j, 0, 0)),
        ],
        out_shape=[
            jax.ShapeDtypeStruct((m, d), F32),
            jax.ShapeDtypeStruct((batch, n_heads, seq, HEAD_DIM + LANES), BF16),
            jax.ShapeDtypeStruct((batch, n_heads, seq, HEAD_DIM), BF16),
            jax.ShapeDtypeStruct((batch, tps, n_heads, nblk, HEAD_DIM), F32),
        ],
        scratch_shapes=[pltpu.VMEM((tm, d), BF16), pltpu.VMEM((tm, d), BF16)],
        compiler_params=_params(2),
        name="qkv",
    )(x, g_kv, g_q, w_kv, w_kv, w_q, k_norm_g, q_norm_g)


def _moba_kernel(q_ref, ka_ref, v_ref, km_ref, o_ref,
                 qa_scr, m_scr, l_scr, acc_scr, pbias_scr, obias_scr, slope_scr, *, n_heads):
    h = pl.program_id(1)
    i = pl.program_id(2)
    tq = MOBA_BLOCK

    @pl.when(i == 0)
    def _():
        hv = (jnp.zeros((tq, 1), jnp.int32) + (h + 1)).astype(F32)
        slope = jnp.exp2(hv * (-8.0 / n_heads))
        rel = (lax.broadcasted_iota(jnp.int32, (tq, tq), 0)
               - lax.broadcasted_iota(jnp.int32, (tq, tq), 1))
        dist = slope * rel.astype(F32)
        pbias_scr[...] = dist
        obias_scr[...] = jnp.where(rel >= 0, dist, -NEG_INF)
        slope_scr[...] = slope

    q = q_ref[...]
    gate = lax.dot_general(q, km_ref[0, 0], (((1,), (1,)), ((), ())),
                           precision=lax.Precision.HIGHEST, preferred_element_type=F32)
    blk = lax.broadcasted_iota(jnp.int32, gate.shape, 1)
    blk_f = blk.astype(F32)
    past = blk < i
    work = jnp.where(past, gate, NEG_INF)
    sel = jnp.zeros(gate.shape, jnp.bool_)
    for _ in range(TOP_K):
        mx = jnp.max(work, axis=-1, keepdims=True)
        idx = jnp.min(jnp.where(work == mx, blk_f, float(LANES)), axis=-1, keepdims=True)
        hit = blk_f == idx
        sel = jnp.logical_or(sel, hit)
        work = jnp.where(hit, -jnp.inf, work)
    allowed = jnp.logical_or(jnp.logical_and(sel, past), blk == i)
    selbias = jnp.where(allowed, 0.0, NEG_INF)

    scale = HEAD_DIM ** -0.5
    qa_scr[:, 0:HEAD_DIM] = (q * scale).astype(BF16)
    qa_scr[:, HEAD_DIM:HEAD_DIM + LANES] = selbias.astype(BF16)
    qa = qa_scr[...]
    slope_col = slope_scr[...]

    def scores(n):
        start = pl.multiple_of(n * tq, tq)
        ka = ka_ref[0, 0, pl.ds(start, tq), :]
        s = lax.dot_general(qa, ka, (((1,), (1,)), ((), ())), preferred_element_type=F32)
        return s, start

    s, start = scores(i)
    s = s - obias_scr[...]
    m0 = jnp.max(s, axis=-1, keepdims=True)
    p = jnp.exp(s - m0)
    m_scr[...] = m0
    l_scr[...] = jnp.sum(p, axis=-1, keepdims=True)
    acc_scr[...] = jnp.dot(p.astype(BF16), v_ref[0, 0, pl.ds(start, tq), :],
                           preferred_element_type=F32)

    def past_block(n, carry):
        s, start = scores(n)
        far = ((i - n) * tq + jnp.zeros((tq, 1), jnp.int32)).astype(F32)
        s = s - pbias_scr[...] - slope_col * far
        m_prev = m_scr[...]
        m_new = jnp.maximum(m_prev, jnp.max(s, axis=-1, keepdims=True))
        alpha = jnp.exp(m_prev - m_new)
        p = jnp.exp(s - m_new)
        l_scr[...] = alpha * l_scr[...] + jnp.sum(p, axis=-1, keepdims=True)
        acc_scr[...] = alpha * acc_scr[...] + jnp.dot(
            p.astype(BF16), v_ref[0, 0, pl.ds(start, tq), :], preferred_element_type=F32)
        m_scr[...] = m_new
        return carry

    lax.fori_loop(0, i, past_block, 0)
    o_ref[...] = (acc_scr[...] / l_scr[...]).astype(o_ref.dtype)


def _moba(q, ka, v, km, *, batch, seq):
    m, d = q.shape
    n_heads = d // HEAD_DIM
    nb = seq // MOBA_BLOCK
    assert seq % MOBA_BLOCK == 0 and nb <= LANES
    tq = MOBA_BLOCK
    kern = functools.partial(_moba_kernel, n_heads=n_heads)
    return pl.pallas_call(
        kern,
        grid=(batch, n_heads, nb),
        in_specs=[
            pl.BlockSpec((tq, HEAD_DIM), lambda b, h, i: (b * nb + i, h)),
            pl.BlockSpec((1, 1, seq, HEAD_DIM + LANES), lambda b, h, i: (b, h, 0, 0)),
            pl.BlockSpec((1, 1, seq, HEAD_DIM), lambda b, h, i: (b, h, 0, 0)),
            pl.BlockSpec((1, 1, LANES, HEAD_DIM), lambda b, h, i: (b, h, 0, 0)),
        ],
        out_specs=pl.BlockSpec((tq, HEAD_DIM), lambda b, h, i: (b * nb + i, h)),
        out_shape=jax.ShapeDtypeStruct((m, d), BF16),
        scratch_shapes=[
            pltpu.VMEM((tq, HEAD_DIM + LANES), BF16),
            pltpu.VMEM((tq, 1), F32),
            pltpu.VMEM((tq, 1), F32),
            pltpu.VMEM((tq, HEAD_DIM), F32),
            pltpu.VMEM((tq, tq), F32),
            pltpu.VMEM((tq, tq), F32),
            pltpu.VMEM((tq, 1), F32),
        ],
        compiler_params=_params(3),
        name="moba",
    )(q, ka, v, km)


def _wo_residual_kernel(o_ref, w_ref, x_ref, out_ref):
    out_ref[...] = x_ref[...] + jnp.dot(o_ref[...], w_ref[...], preferred_element_type=F32)


def _wo_residual(o, w, x, *, tm, tn):
    m, d = x.shape
    return pl.pallas_call(
        _wo_residual_kernel,
        grid=(m // tm, d // tn),
        in_specs=[
            pl.BlockSpec((tm, d), lambda i, j: (i, 0)),
            pl.BlockSpec((d, tn), lambda i, j: (0, j)),
            pl.BlockSpec((tm, tn), lambda i, j: (i, j)),
        ],
        out_specs=pl.BlockSpec((tm, tn), lambda i, j: (i, j)),
        out_shape=jax.ShapeDtypeStruct((m, d), F32),
        compiler_params=_params(2),
        name="wo_residual",
    )(o, w, x)


def _tiles(seq, d, f):
    def pick(n, cands):
        for c in cands:
            if n % c == 0:
                return c
        raise ValueError(f"no tile for {n}")
    return dict(
        tm=pick(seq, (512, 256)),
        tm_conv=pick(seq, (256,)),
        tn=pick(d, (512, 256, 128)),
        tf=pick(f, (512, 256, 128)),
    )


def kernel(x, conv_norm_g, conv_w_pw1, conv_b_pw1, conv_w_dw, conv_b_dw, conv_ln_g, conv_ln_b,
           conv_w_pw2, conv_b_pw2, kv_norm_g, w_kv, k_norm_g, attn_norm_g, w_q, q_norm_g, w_o,
           ffn_norm_g, ffn_w_in, ffn_w_conv, ffn_w_out):
    batch, seq, d = x.shape
    depth = ffn_w_in.shape[0]
    n_a = conv_w_pw1.shape[0]
    f = ffn_w_out.shape[1]
    t = _tiles(seq, d, f)
    n_heads = d // HEAD_DIM
    nb = seq // MOBA_BLOCK
    row = lambda v: v.reshape(1, -1)

    xf = x.reshape(batch * seq, d)
    kv = None
    for layer in range(depth):
        if layer < n_a:
            a = layer
            u = _pw1_glu(xf, row(conv_norm_g[a]), conv_w_pw1[a].astype(BF16), row(conv_b_pw1[a]),
                         tm=t["tm"], tn=t["tn"])
            xf = _conv_ln_pw2(u, xf, conv_w_dw[a], row(conv_b_dw[a]), row(conv_ln_g[a]),
                              row(conv_ln_b[a]), conv_w_pw2[a].astype(BF16), row(conv_b_pw2[a]),
                              tm=t["tm_conv"], seq=seq)
        else:
            b = layer - n_a
            q, ka, v, km = _qkv(xf, row(kv_norm_g), row(attn_norm_g[b]), w_kv.astype(BF16),
                                w_q[b].astype(BF16), row(k_norm_g), row(q_norm_g[b]),
                                tm=t["tm"], tn=t["tn"], batch=batch, seq=seq)
            if kv is None:
                km = km.transpose(0, 2, 1, 3, 4).reshape(batch, n_heads, nb, HEAD_DIM)
                km = jnp.pad(km, ((0, 0), (0, 0), (0, LANES - nb), (0, 0)))
                kv = (ka, v, km)
            ka, v, km = kv
            o = _moba(q, ka, v, km, batch=batch, seq=seq)
            xf = _wo_residual(o, w_o[b].astype(BF16), xf, tm=t["tm"], tn=t["tn"])
        xf = _conv_ffn(xf, row(ffn_norm_g[layer]), ffn_w_in[layer].astype(BF16), ffn_w_conv[layer],
                       ffn_w_out[layer].astype(BF16), tm=t["tm"], tf=t["tf"], seq=seq)
    return xf.reshape(batch, seq, d)
```

```python
import functools

import jax
import jax.numpy as jnp
from jax import lax
from jax.experimental import pallas as pl
from jax.experimental.pallas import tpu as pltpu

EPS = 1e-6
NEG_INF = -1e30
HEAD_DIM = 128
MOBA_BLOCK = 256
TOP_K = 3
LANES = 128
SUBLANES = 8
BF16_SUBLANES = 16
VT_PAD = BF16_SUBLANES
VMEM_LIMIT_BYTES = 56 * 1024 * 1024

F32 = jnp.float32
BF16 = jnp.bfloat16


def _params(n_axes):
    return pltpu.CompilerParams(dimension_semantics=("arbitrary",) * n_axes,
                                vmem_limit_bytes=VMEM_LIMIT_BYTES)


def _rms_normalize(x):
    return x * lax.rsqrt(jnp.mean(x * x, axis=-1, keepdims=True) + EPS)


def _silu(x):
    return x * jax.nn.sigmoid(x)


def _pw1_glu_kernel(x_ref, g_ref, wa_ref, wg_ref, ba_ref, bg_ref, u_ref, h_scr):
    @pl.when(pl.program_id(1) == 0)
    def _():
        h_scr[...] = (_rms_normalize(x_ref[...]) * g_ref[...]).astype(BF16)

    h = h_scr[...]
    a = jnp.dot(h, wa_ref[...], preferred_element_type=F32) + ba_ref[...]
    gate = jnp.dot(h, wg_ref[...], preferred_element_type=F32) + bg_ref[...]
    u_ref[...] = a * jax.nn.sigmoid(gate)


def _pw1_glu(x, gain, w1, b1, *, tm, tn):
    m, d = x.shape
    nj = d // tn
    return pl.pallas_call(
        _pw1_glu_kernel,
        grid=(m // tm, nj),
        in_specs=[
            pl.BlockSpec((tm, d), lambda i, j: (i, 0)),
            pl.BlockSpec((1, d), lambda i, j: (0, 0)),
            pl.BlockSpec((d, tn), lambda i, j: (0, j)),
            pl.BlockSpec((d, tn), lambda i, j: (0, nj + j)),
            pl.BlockSpec((1, tn), lambda i, j: (0, j)),
            pl.BlockSpec((1, tn), lambda i, j: (0, nj + j)),
        ],
        out_specs=pl.BlockSpec((tm, tn), lambda i, j: (i, j)),
        out_shape=jax.ShapeDtypeStruct((m, d), F32),
        scratch_shapes=[pltpu.VMEM((tm, d), BF16)],
        compiler_params=_params(2),
        name="pw1_glu",
    )(x, gain, w1, w1, b1, b1)


CONV_HALO = 32
CONV_ROWS = 64


def _conv_ln_pw2_kernel(u_ref, uh_ref, x_ref, wdw_ref, bdw_ref, lng_ref, lnb_ref, w2_ref, b2_ref,
                        o_ref, ub_scr, y_scr, *, tm, seq, width):
    i = pl.program_id(0)
    d = u_ref.shape[1]
    first = (i * tm) % seq == 0

    @pl.when(first)
    def _():
        ub_scr[0:CONV_HALO, :] = jnp.zeros((CONV_HALO, d), F32)

    @pl.when(jnp.logical_not(first))
    def _():
        ub_scr[0:CONV_HALO, :] = uh_ref[...]

    ub_scr[CONV_HALO:CONV_HALO + tm, :] = u_ref[...]

    off0 = CONV_HALO - (width - 1)
    max_off = off0 + width - 1
    n_a = max_off // SUBLANES + 1
    win_rows = CONV_ROWS + (n_a - 1) * SUBLANES

    def strip(c, carry):
        col = pl.multiple_of(c * LANES, LANES)
        for r in range(tm // CONV_ROWS):
            base = r * CONV_ROWS
            acc = jnp.zeros((CONV_ROWS, LANES), F32)
            for b in range(SUBLANES):
                rows_needed = win_rows if b == 0 else win_rows - SUBLANES
                win = ub_scr[pl.ds(base + b, rows_needed), pl.ds(col, LANES)]
                for a in range(n_a):
                    off = SUBLANES * a + b
                    k = off - off0
                    if 0 <= k < width and SUBLANES * a + CONV_ROWS <= rows_needed:
                        wk = wdw_ref[pl.ds(k, 1), pl.ds(col, LANES)]
                        acc = acc + wk * win[SUBLANES * a:SUBLANES * a + CONV_ROWS, :]
            y_scr[pl.ds(base, CONV_ROWS), pl.ds(col, LANES)] = acc
        return carry

    lax.fori_loop(0, d // LANES, strip, 0)

    y = y_scr[...] + bdw_ref[...]
    mu = jnp.mean(y, axis=-1, keepdims=True)
    yc = y - mu
    var = jnp.mean(yc * yc, axis=-1, keepdims=True)
    yn = yc * lax.rsqrt(var + EPS) * lng_ref[...] + lnb_ref[...]
    act = _silu(yn).astype(BF16)
    o_ref[...] = x_ref[...] + (jnp.dot(act, w2_ref[...], preferred_element_type=F32) + b2_ref[...])


def _conv_ln_pw2(u, x, w_dw, b_dw, ln_g, ln_b, w2, b2, *, tm, seq):
    m, d = u.shape
    width = w_dw.shape[0]
    assert width - 1 <= CONV_HALO and tm % CONV_ROWS == 0 and tm % CONV_HALO == 0
    hb = tm // CONV_HALO
    kern = functools.partial(_conv_ln_pw2_kernel, tm=tm, seq=seq, width=width)
    row = pl.BlockSpec((1, d), lambda i: (0, 0))
    return pl.pallas_call(
        kern,
        grid=(m // tm,),
        in_specs=[
            pl.BlockSpec((tm, d), lambda i: (i, 0)),
            pl.BlockSpec((CONV_HALO, d), lambda i: (jnp.maximum(i * hb - 1, 0), 0)),
            pl.BlockSpec((tm, d), lambda i: (i, 0)),
            pl.BlockSpec((width, d), lambda i: (0, 0)),
            row, row, row,
            pl.BlockSpec((d, d), lambda i: (0, 0)),
            row,
        ],
        out_specs=pl.BlockSpec((tm, d), lambda i: (i, 0)),
        out_shape=jax.ShapeDtypeStruct((m, d), F32),
        scratch_shapes=[pltpu.VMEM((CONV_HALO + tm, d), F32), pltpu.VMEM((tm, d), F32)],
        compiler_params=_params(1),
        name="conv_ln_pw2",
    )(u, u, x, w_dw, b_dw, ln_g, ln_b, w2, b2)


def _conv_ffn_kernel(x_ref, g_ref, wg_ref, wu_ref, wc_ref, wo_ref, o_ref,
                     h_scr, acc_scr, gs_scr, tail_scr, *, tm, seq):
    i = pl.program_id(0)
    j = pl.program_id(1)
    nj = pl.num_programs(1)
    tf = wg_ref.shape[1]
    first = (i * tm) % seq == 0

    @pl.when(j == 0)
    def _():
        h_scr[...] = (_rms_normalize(x_ref[...]) * g_ref[...]).astype(BF16)

    h = h_scr[...]
    g = jnp.dot(h, wg_ref[...], preferred_element_type=F32)
    u = jnp.dot(h, wu_ref[...], preferred_element_type=F32)

    @pl.when(first)
    def _():
        gs_scr[0:SUBLANES, :] = jnp.zeros((SUBLANES, tf), F32)

    @pl.when(jnp.logical_not(first))
    def _():
        gs_scr[0:SUBLANES, :] = tail_scr[j]

    gs_scr[SUBLANES:SUBLANES + tm, :] = g
    tail_scr[j] = g[tm - SUBLANES:tm, :]
    g1 = gs_scr[pl.ds(SUBLANES - 1, tm), :]
    g2 = gs_scr[pl.ds(SUBLANES - 2, tm), :]
    gc = wc_ref[2:3, :] * g + wc_ref[1:2, :] * g1 + wc_ref[0:1, :] * g2
    act = (_silu(gc) * u).astype(BF16)
    part = jnp.dot(act, wo_ref[...], preferred_element_type=F32)

    @pl.when(j == 0)
    def _():
        acc_scr[...] = part

    @pl.when(j > 0)
    def _():
        acc_scr[...] += part

    @pl.when(j == nj - 1)
    def _():
        o_ref[...] = x_ref[...] + acc_scr[...]


def _conv_ffn(x, gain, w_in, w_conv, w_out, *, tm, tf, seq):
    m, d = x.shape
    f = w_out.shape[0]
    assert w_conv.shape[0] == 3 and f % tf == 0
    nj = f // tf
    kern = functools.partial(_conv_ffn_kernel, tm=tm, seq=seq)
    return pl.pallas_call(
        kern,
        grid=(m // tm, nj),
        in_specs=[
            pl.BlockSpec((tm, d), lambda i, j: (i, 0)),
            pl.BlockSpec((1, d), lambda i, j: (0, 0)),
            pl.BlockSpec((d, tf), lambda i, j: (0, j)),
            pl.BlockSpec((d, tf), lambda i, j: (0, nj + j)),
            pl.BlockSpec((3, tf), lambda i, j: (0, j)),
            pl.BlockSpec((tf, d), lambda i, j: (j, 0)),
        ],
        out_specs=pl.BlockSpec((tm, d), lambda i, j: (i, 0)),
        out_shape=jax.ShapeDtypeStruct((m, d), F32),
        scratch_shapes=[
            pltpu.VMEM((tm, d), BF16),
            pltpu.VMEM((tm, d), F32),
            pltpu.VMEM((SUBLANES + tm, tf), F32),
            pltpu.VMEM((nj, SUBLANES, tf), F32),
        ],
        compiler_params=_params(2),
        name="conv_ffn",
    )(x, gain, w_in, w_in, w_conv, w_out)


def _qkv_kernel(x_ref, gkv_ref, gq_ref, wk_ref, wv_ref, wq_ref, kng_ref, qng_ref,
                qt_ref, ka_ref, vt_ref, km_ref, hkv_scr, hq_scr, *, tm, seq, hps):
    i = pl.program_id(0)

    @pl.when(pl.program_id(1) == 0)
    def _():
        xn = _rms_normalize(x_ref[...])
        hkv_scr[...] = (xn * gkv_ref[...]).astype(BF16)
        hq_scr[...] = (xn * gq_ref[...]).astype(BF16)

    hkv = hkv_scr[...]
    kk = jnp.dot(hkv, wk_ref[...], preferred_element_type=F32)
    vv = jnp.dot(hkv, wv_ref[...], preferred_element_type=F32)
    qq = jnp.dot(hq_scr[...], wq_ref[...], preferred_element_type=F32)

    nblk = tm // MOBA_BLOCK
    blk0 = ((i * tm) % seq) // MOBA_BLOCK
    shift = MOBA_BLOCK.bit_length() - 1
    row_blk = blk0 + lax.shift_right_logical(
        lax.broadcasted_iota(jnp.int32, (tm, LANES), 0), jnp.full((tm, LANES), shift, jnp.int32))
    lane = lax.broadcasted_iota(jnp.int32, (tm, LANES), 1)
    onehot = (lane == row_blk).astype(BF16)
    ones_row = (lax.broadcasted_iota(jnp.int32, (VT_PAD, MOBA_BLOCK), 0) == 0).astype(BF16)

    for hh in range(hps):
        sl = slice(hh * HEAD_DIM, (hh + 1) * HEAD_DIM)
        kn = _rms_normalize(kk[:, sl]) * kng_ref[...]
        qt_ref[0, hh] = (_rms_normalize(qq[:, sl]) * qng_ref[...]).T
        ka_ref[0, hh, :, 0:HEAD_DIM] = kn.astype(BF16)
        ka_ref[0, hh, :, HEAD_DIM:HEAD_DIM + LANES] = onehot
        vt = vv[:, sl].T.astype(BF16)
        for bb in range(nblk):
            cols = slice(bb * MOBA_BLOCK, (bb + 1) * MOBA_BLOCK)
            vt_ref[0, hh, bb, 0:HEAD_DIM, :] = vt[:, cols]
            vt_ref[0, hh, bb, HEAD_DIM:HEAD_DIM + VT_PAD, :] = ones_row
            km_ref[0, 0, hh, bb:bb + 1, :] = jnp.mean(kn[cols, :], axis=0, keepdims=True)


def _qkv(x, g_kv, g_q, w_kv, w_q, k_norm_g, q_norm_g, *, tm, tn, batch, seq):
    m, d = x.shape
    assert tn % HEAD_DIM == 0 and tm % MOBA_BLOCK == 0 and seq % tm == 0
    nj = d // tn
    hps = tn // HEAD_DIM
    n_heads = d // HEAD_DIM
    tps = seq // tm
    nblk = tm // MOBA_BLOCK
    kern = functools.partial(_qkv_kernel, tm=tm, seq=seq, hps=hps)
    row_d = pl.BlockSpec((1, d), lambda i, j: (0, 0))
    row_h = pl.BlockSpec((1, HEAD_DIM), lambda i, j: (0, 0))
    return pl.pallas_call(
        kern,
        grid=(m // tm, nj),
        in_specs=[
            pl.BlockSpec((tm, d), lambda i, j: (i, 0)),
            row_d, row_d,
            pl.BlockSpec((d, tn), lambda i, j: (0, j)),
            pl.BlockSpec((d, tn), lambda i, j: (0, nj + j)),
            pl.BlockSpec((d, tn), lambda i, j: (0, j)),
            row_h, row_h,
        ],
        out_specs=[
            pl.BlockSpec((1, hps, HEAD_DIM, tm), lambda i, j: (i // tps, j, 0, i % tps)),
            pl.BlockSpec((1, hps, tm, HEAD_DIM + LANES), lambda i, j: (i // tps, j, i % tps, 0)),
            pl.BlockSpec((1, hps, nblk, HEAD_DIM + VT_PAD, MOBA_BLOCK),
                         lambda i, j: (i // tps, j, i % tps, 0, 0)),
            pl.BlockSpec((1, 1, hps, nblk, HEAD_DIM), lambda i, j: (i // tps, i % tps, j, 0, 0)),
        ],
        out_shape=[
            jax.ShapeDtypeStruct((batch, n_heads, HEAD_DIM, seq), F32),
            jax.ShapeDtypeStruct((batch, n_heads, seq, HEAD_DIM + LANES), BF16),
            jax.ShapeDtypeStruct((batch, n_heads, seq // MOBA_BLOCK, HEAD_DIM + VT_PAD, MOBA_BLOCK),
                                 BF16),
            jax.ShapeDtypeStruct((batch, tps, n_heads, nblk, HEAD_DIM), F32),
        ],
        scratch_shapes=[pltpu.VMEM((tm, d), BF16), pltpu.VMEM((tm, d), BF16)],
        compiler_params=_params(2),
        name="qkv",
    )(x, g_kv, g_q, w_kv, w_kv, w_q, k_norm_g, q_norm_g)


SELECT_ROWS = 1024
MOBA_HEADS = 4


def _select_kernel(qt_ref, km_ref, qat_ref):
    rows = qt_ref.shape[3]
    nbp = km_ref.shape[2]
    qt = qt_ref[0, 0]
    gate = jnp.dot(km_ref[0, 0], qt, precision=lax.Precision.HIGHEST,
                   preferred_element_type=F32)
    blk = lax.broadcasted_iota(jnp.int32, gate.shape, 0)
    blk_f = blk.astype(F32)
    shift = MOBA_BLOCK.bit_length() - 1
    cur = pl.program_id(2) * (rows // MOBA_BLOCK) + lax.shift_right_logical(
        lax.broadcasted_iota(jnp.int32, gate.shape, 1), jnp.full(gate.shape, shift, jnp.int32))
    past = blk < cur
    work = jnp.where(past, gate, NEG_INF)
    sel = jnp.zeros(gate.shape, jnp.bool_)
    for _ in range(TOP_K):
        mx = jnp.max(work, axis=0, keepdims=True)
        idx = jnp.min(jnp.where(work == mx, blk_f, float(LANES)), axis=0, keepdims=True)
        hit = blk_f == idx
        sel = jnp.logical_or(sel, hit)
        work = jnp.where(hit, -jnp.inf, work)
    allowed = jnp.logical_or(jnp.logical_and(sel, past), blk == cur)
    selbias = jnp.where(allowed, 0.0, NEG_INF)
    qat_ref[0, 0, 0:HEAD_DIM, :] = (qt * HEAD_DIM ** -0.5).astype(BF16)
    qat_ref[0, 0, HEAD_DIM:HEAD_DIM + nbp, :] = selbias.astype(BF16)
    if nbp < LANES:
        qat_ref[0, 0, HEAD_DIM + nbp:HEAD_DIM + LANES, :] = jnp.zeros((LANES - nbp, rows), BF16)


def _select(qt, km, *, seq):
    batch, n_heads, _, _ = qt.shape
    nbp = km.shape[2]
    rows = min(SELECT_ROWS, seq)
    assert seq % rows == 0 and rows % MOBA_BLOCK == 0 and nbp % 16 == 0 and nbp <= LANES
    return pl.pallas_call(
        _select_kernel,
        grid=(batch, n_heads, seq // rows),
        in_specs=[
            pl.BlockSpec((1, 1, HEAD_DIM, rows), lambda b, h, r: (b, h, 0, r)),
            pl.BlockSpec((1, 1, nbp, HEAD_DIM), lambda b, h, r: (b, h, 0, 0)),
        ],
        out_specs=pl.BlockSpec((1, 1, HEAD_DIM + LANES, rows), lambda b, h, r: (b, h, 0, r)),
        out_shape=jax.ShapeDtypeStruct((batch, n_heads, HEAD_DIM + LANES, seq), BF16),
        compiler_params=_params(3),
        name="select",
    )(qt, km)


def _moba_kernel(qat_ref, ka_ref, vt_ref, o_ref, acc_scr, m_scr, bias_scr, slope_scr, *, n_heads, hb):
    hg = pl.program_id(1)
    i = pl.program_id(2)
    tq = MOBA_BLOCK

    @pl.when(i == 0)
    def _():
        rel = (lax.broadcasted_iota(jnp.int32, (tq, tq), 1)
               - lax.broadcasted_iota(jnp.int32, (tq, tq), 0))
        rel_f = rel.astype(F32)
        for hh in range(hb):
            hv = (jnp.zeros((1, tq), jnp.int32) + (hg * hb + hh + 1)).astype(F32)
            slope = jnp.exp2(hv * (-8.0 / n_heads))
            dist = slope * rel_f
            bias_scr[hh, 0] = dist
            bias_scr[hh, 1] = jnp.where(rel >= 0, dist, -NEG_INF)
            slope_scr[hh] = slope

    for hh in range(hb):
        m_scr[hh] = jnp.full((1, tq), NEG_INF, F32)
        acc_scr[hh] = jnp.zeros((HEAD_DIM + VT_PAD, tq), F32)

    def block(it, carry):
        n = i - it
        start = pl.multiple_of(n * tq, tq)
        own = jnp.where(it == 0, 1, 0)
        far = (it * tq + jnp.zeros((1, tq), jnp.int32)).astype(F32)
        heads = range(hb)
        s = [jnp.dot(ka_ref[0, hh, pl.ds(start, tq), :], qat_ref[0, hh],
                     preferred_element_type=F32) - bias_scr[hh, own] for hh in heads]
        c = [slope_scr[hh] * far for hh in heads]
        m_prev = [m_scr[hh] for hh in heads]
        m_new = [jnp.maximum(m_prev[hh], jnp.max(s[hh], axis=0, keepdims=True) - c[hh])
                 for hh in heads]
        alpha = [jnp.exp(m_prev[hh] - m_new[hh]) for hh in heads]
        p = [jnp.exp(s[hh] - (m_new[hh] + c[hh])).astype(BF16) for hh in heads]
        for hh in heads:
            acc_scr[hh] = alpha[hh] * acc_scr[hh] + jnp.dot(
                vt_ref[0, hh, n], p[hh], preferred_element_type=F32)
            m_scr[hh] = m_new[hh]
        return carry

    lax.fori_loop(0, i + 1, block, 0)
    for hh in range(hb):
        acc = acc_scr[hh]
        ot = acc[0:HEAD_DIM, :] / acc[HEAD_DIM:HEAD_DIM + 1, :]
        o_ref[:, hh * HEAD_DIM:(hh + 1) * HEAD_DIM] = ot.T.astype(o_ref.dtype)


def _moba(qat, ka, vt, *, seq):
    batch, n_heads, wide, _ = qat.shape
    d = n_heads * HEAD_DIM
    nb = seq // MOBA_BLOCK
    hb = MOBA_HEADS if n_heads % MOBA_HEADS == 0 else 1
    tq = MOBA_BLOCK
    kern = functools.partial(_moba_kernel, n_heads=n_heads, hb=hb)
    return pl.pallas_call(
        kern,
        grid=(batch, n_heads // hb, nb),
        in_specs=[
            pl.BlockSpec((1, hb, wide, tq), lambda b, g, i: (b, g, 0, i)),
            pl.BlockSpec((1, hb, seq, wide), lambda b, g, i: (b, g, 0, 0)),
            pl.BlockSpec((1, hb, nb, HEAD_DIM + VT_PAD, tq), lambda b, g, i: (b, g, 0, 0, 0)),
        ],
        out_specs=pl.BlockSpec((tq, hb * HEAD_DIM), lambda b, g, i: (b * nb + i, g)),
        out_shape=jax.ShapeDtypeStruct((batch * seq, d), BF16),
        scratch_shapes=[
            pltpu.VMEM((hb, HEAD_DIM + VT_PAD, tq), F32),
            pltpu.VMEM((hb, 1, tq), F32),
            pltpu.VMEM((hb, 2, tq, tq), F32),
            pltpu.VMEM((hb, 1, tq), F32),
        ],
        compiler_params=_params(3),
        name="moba",
    )(qat, ka, vt)


def _wo_residual_kernel(o_ref, w_ref, x_ref, out_ref):
    out_ref[...] = x_ref[...] + jnp.dot(o_ref[...], w_ref[...], preferred_element_type=F32)


def _wo_residual(o, w, x, *, tm, tn):
    m, d = x.shape
    return pl.pallas_call(
        _wo_residual_kernel,
        grid=(m // tm, d // tn),
        in_specs=[
            pl.BlockSpec((tm, d), lambda i, j: (i, 0)),
            pl.BlockSpec((d, tn), lambda i, j: (0, j)),
            pl.BlockSpec((tm, tn), lambda i, j: (i, j)),
        ],
        out_specs=pl.BlockSpec((tm, tn), lambda i, j: (i, j)),
        out_shape=jax.ShapeDtypeStruct((m, d), F32),
        compiler_params=_params(2),
        name="wo_residual",
    )(o, w, x)


def _tiles(seq, d, f):
    def pick(n, cands):
        for c in cands:
            if n % c == 0:
                return c
        raise ValueError(f"no tile for {n}")
    return dict(
        tm=pick(seq, (512, 256)),
        tm_conv=pick(seq, (256,)),
        tn=pick(d, (512, 256, 128)),
        tf=pick(f, (512, 256, 128)),
    )


def kernel(x, conv_norm_g, conv_w_pw1, conv_b_pw1, conv_w_dw, conv_b_dw, conv_ln_g, conv_ln_b,
           conv_w_pw2, conv_b_pw2, kv_norm_g, w_kv, k_norm_g, attn_norm_g, w_q, q_norm_g, w_o,
           ffn_norm_g, ffn_w_in, ffn_w_conv, ffn_w_out):
    batch, seq, d = x.shape
    depth = ffn_w_in.shape[0]
    n_a = conv_w_pw1.shape[0]
    f = ffn_w_out.shape[1]
    t = _tiles(seq, d, f)
    n_heads = d // HEAD_DIM
    nb = seq // MOBA_BLOCK
    row = lambda v: v.reshape(1, -1)

    xf = x.reshape(batch * seq, d)
    kv = None
    for layer in range(depth):
        if layer < n_a:
            a = layer
            u = _pw1_glu(xf, row(conv_norm_g[a]), conv_w_pw1[a].astype(BF16), row(conv_b_pw1[a]),
                         tm=t["tm"], tn=t["tn"])
            xf = _conv_ln_pw2(u, xf, conv_w_dw[a], row(conv_b_dw[a]), row(conv_ln_g[a]),
                              row(conv_ln_b[a]), conv_w_pw2[a].astype(BF16), row(conv_b_pw2[a]),
                              tm=t["tm_conv"], seq=seq)
        else:
            b = layer - n_a
            qt, ka, vt, km = _qkv(xf, row(kv_norm_g), row(attn_norm_g[b]), w_kv.astype(BF16),
                                  w_q[b].astype(BF16), row(k_norm_g), row(q_norm_g[b]),
                                  tm=t["tm"], tn=t["tn"], batch=batch, seq=seq)
            if kv is None:
                km = km.transpose(0, 2, 1, 3, 4).reshape(batch, n_heads, nb, HEAD_DIM)
                km = jnp.pad(km, ((0, 0), (0, 0), (0, -nb % BF16_SUBLANES), (0, 0)))
                kv = (ka, vt, km)
            ka, vt, km = kv
            qat = _select(qt, km, seq=seq)
            o = _moba(qat, ka, vt, seq=seq)
            xf = _wo_residual(o, w_o[b].astype(BF16), xf, tm=t["tm"], tn=t["tn"])
        xf = _conv_ffn(xf, row(ffn_norm_g[layer]), ffn_w_in[layer].astype(BF16), ffn_w_conv[layer],
                       ffn_w_out[layer].astype(BF16), tm=t["tm"], tf=t["tf"], seq=seq)
    return xf.reshape(batch, seq, d)
```

```python
import functools

import jax
import jax.numpy as jnp
from jax import lax
from jax.experimental import pallas as pl
from jax.experimental.pallas import tpu as pltpu

EPS = 1e-6
NEG_INF = -1e30
HEAD_DIM = 128
MOBA_BLOCK = 256
TOP_K = 3
LANES = 128
SUBLANES = 8
BF16_SUBLANES = 16
VT_PAD = BF16_SUBLANES
ALIBI_TERMS = 3


def _padded_blocks(seq):
    nb = seq // MOBA_BLOCK
    return -(-nb // BF16_SUBLANES) * BF16_SUBLANES
VMEM_LIMIT_BYTES = 56 * 1024 * 1024

F32 = jnp.float32
BF16 = jnp.bfloat16


def _params(n_axes):
    return pltpu.CompilerParams(dimension_semantics=("arbitrary",) * n_axes,
                                vmem_limit_bytes=VMEM_LIMIT_BYTES)


def _rms_normalize(x):
    return x * lax.rsqrt(jnp.mean(x * x, axis=-1, keepdims=True) + EPS)


def _silu(x):
    return x * jax.nn.sigmoid(x)


def _pw1_glu_kernel(x_ref, g_ref, wa_ref, wg_ref, ba_ref, bg_ref, u_ref, h_scr):
    @pl.when(pl.program_id(1) == 0)
    def _():
        h_scr[...] = (_rms_normalize(x_ref[...]) * g_ref[...]).astype(BF16)

    h = h_scr[...]
    a = jnp.dot(h, wa_ref[...], preferred_element_type=F32) + ba_ref[...]
    gate = jnp.dot(h, wg_ref[...], preferred_element_type=F32) + bg_ref[...]
    u_ref[...] = a * jax.nn.sigmoid(gate)


def _pw1_glu(x, gain, w1, b1, *, tm, tn):
    m, d = x.shape
    nj = d // tn
    return pl.pallas_call(
        _pw1_glu_kernel,
        grid=(m // tm, nj),
        in_specs=[
            pl.BlockSpec((tm, d), lambda i, j: (i, 0)),
            pl.BlockSpec((1, d), lambda i, j: (0, 0)),
            pl.BlockSpec((d, tn), lambda i, j: (0, j)),
            pl.BlockSpec((d, tn), lambda i, j: (0, nj + j)),
            pl.BlockSpec((1, tn), lambda i, j: (0, j)),
            pl.BlockSpec((1, tn), lambda i, j: (0, nj + j)),
        ],
        out_specs=pl.BlockSpec((tm, tn), lambda i, j: (i, j)),
        out_shape=jax.ShapeDtypeStruct((m, d), F32),
        scratch_shapes=[pltpu.VMEM((tm, d), BF16)],
        compiler_params=_params(2),
        name="pw1_glu",
    )(x, gain, w1, w1, b1, b1)


CONV_HALO = 32
CONV_ROWS = 64


def _conv_ln_pw2_kernel(u_ref, uh_ref, x_ref, wdw_ref, bdw_ref, lng_ref, lnb_ref, w2_ref, b2_ref,
                        o_ref, ub_scr, y_scr, *, tm, seq, width):
    i = pl.program_id(0)
    d = u_ref.shape[1]
    first = (i * tm) % seq == 0

    @pl.when(first)
    def _():
        ub_scr[0:CONV_HALO, :] = jnp.zeros((CONV_HALO, d), F32)

    @pl.when(jnp.logical_not(first))
    def _():
        ub_scr[0:CONV_HALO, :] = uh_ref[...]

    ub_scr[CONV_HALO:CONV_HALO + tm, :] = u_ref[...]

    off0 = CONV_HALO - (width - 1)
    max_off = off0 + width - 1
    n_a = max_off // SUBLANES + 1
    z_rows = CONV_ROWS + SUBLANES
    win_rows = z_rows + (n_a - 1) * SUBLANES

    def strip(c, carry):
        col = pl.multiple_of(c * LANES, LANES)
        for r in range(tm // CONV_ROWS):
            base = r * CONV_ROWS
            rows = min(win_rows, CONV_HALO + tm - base)
            win = ub_scr[pl.ds(base, rows), pl.ds(col, LANES)]
            y = None
            for b in range(SUBLANES):
                z = None
                for a in range(n_a):
                    k = SUBLANES * a + b - off0
                    if 0 <= k < width:
                        lo = SUBLANES * a
                        hi = min(lo + z_rows, rows)
                        term = wdw_ref[pl.ds(k, 1), pl.ds(col, LANES)] * win[lo:hi, :]
                        if hi - lo < z_rows:
                            term = jnp.concatenate(
                                [term, jnp.zeros((z_rows - (hi - lo), LANES), F32)], axis=0)
                        z = term if z is None else z + term
                part = z[b:b + CONV_ROWS, :]
                y = part if y is None else y + part
            y_scr[pl.ds(base, CONV_ROWS), pl.ds(col, LANES)] = y
        return carry

    lax.fori_loop(0, d // LANES, strip, 0)

    y = y_scr[...] + bdw_ref[...]
    mu = jnp.mean(y, axis=-1, keepdims=True)
    yc = y - mu
    var = jnp.mean(yc * yc, axis=-1, keepdims=True)
    yn = yc * lax.rsqrt(var + EPS) * lng_ref[...] + lnb_ref[...]
    act = _silu(yn).astype(BF16)
    o_ref[...] = x_ref[...] + (jnp.dot(act, w2_ref[...], preferred_element_type=F32) + b2_ref[...])


def _conv_ln_pw2(u, x, w_dw, b_dw, ln_g, ln_b, w2, b2, *, tm, seq):
    m, d = u.shape
    width = w_dw.shape[0]
    assert width - 1 <= CONV_HALO and tm % CONV_ROWS == 0 and tm % CONV_HALO == 0
    hb = tm // CONV_HALO
    kern = functools.partial(_conv_ln_pw2_kernel, tm=tm, seq=seq, width=width)
    row = pl.BlockSpec((1, d), lambda i: (0, 0))
    return pl.pallas_call(
        kern,
        grid=(m // tm,),
        in_specs=[
            pl.BlockSpec((tm, d), lambda i: (i, 0)),
            pl.BlockSpec((CONV_HALO, d), lambda i: (jnp.maximum(i * hb - 1, 0), 0)),
            pl.BlockSpec((tm, d), lambda i: (i, 0)),
            pl.BlockSpec((width, d), lambda i: (0, 0)),
            row, row, row,
            pl.BlockSpec((d, d), lambda i: (0, 0)),
            row,
        ],
        out_specs=pl.BlockSpec((tm, d), lambda i: (i, 0)),
        out_shape=jax.ShapeDtypeStruct((m, d), F32),
        scratch_shapes=[pltpu.VMEM((CONV_HALO + tm, d), F32), pltpu.VMEM((tm, d), F32)],
        compiler_params=_params(1),
        name="conv_ln_pw2",
    )(u, u, x, w_dw, b_dw, ln_g, ln_b, w2, b2)


FFN_ROW_CHUNKS = 4


def _conv_ffn_kernel(x_ref, g_ref, wg_ref, wu_ref, wc_ref, wo_ref, o_ref,
                     h_scr, act_scr, g_scr, u_scr, tail_scr, *, tm, seq, nj, tn):
    i = pl.program_id(0)
    j = pl.program_id(1)
    tf = wg_ref.shape[1]
    first = (i * tm) % seq == 0
    rw = tm // FFN_ROW_CHUNKS

    def project(slot, c):
        rows = slice(c * rw, (c + 1) * rw)
        h = h_scr[rows, :]
        g_scr[slot, SUBLANES + c * rw:SUBLANES + (c + 1) * rw, :] = jnp.dot(
            h, wg_ref[...], preferred_element_type=F32)
        u_scr[slot, rows, :] = jnp.dot(h, wu_ref[...], preferred_element_type=F32)

    def activate(jj, slot, c):
        if c == 0:
            g_scr[slot, 0:SUBLANES, :] = jnp.where(first, 0.0, tail_scr[jj])
        g = g_scr[slot, pl.ds(SUBLANES + c * rw, rw), :]
        g1 = g_scr[slot, pl.ds(SUBLANES - 1 + c * rw, rw), :]
        g2 = g_scr[slot, pl.ds(SUBLANES - 2 + c * rw, rw), :]
        if c == FFN_ROW_CHUNKS - 1:
            tail_scr[jj] = g_scr[slot, pl.ds(tm, SUBLANES), :]
        gc = wc_ref[2:3, :] * g + wc_ref[1:2, :] * g1 + wc_ref[0:1, :] * g2
        act = (_silu(gc) * u_scr[slot, c * rw:(c + 1) * rw, :]).astype(BF16)
        act_scr[c * rw:(c + 1) * rw, pl.ds(pl.multiple_of(jj * tf, tf), tf)] = act

    @pl.when(jnp.logical_and(i == 0, j == 0))
    def _():
        tail_scr[...] = jnp.zeros(tail_scr.shape, F32)

    @pl.when(j == 0)
    def _():
        h_scr[...] = (_rms_normalize(x_ref[...]) * g_ref[...]).astype(BF16)
        for c in range(FFN_ROW_CHUNKS):
            project(0, c)

    @pl.when(jnp.logical_and(j >= 1, j < nj))
    def _():
        slot = j % 2
        for c in range(FFN_ROW_CHUNKS):
            project(slot, c)
            activate(j - 1, 1 - slot, c)

    @pl.when(j == nj)
    def _():
        for c in range(FFN_ROW_CHUNKS):
            activate(nj - 1, (nj - 1) % 2, c)

    @pl.when(j >= nj)
    def _():
        col = pl.multiple_of((j - nj) * tn, tn)
        o_ref[...] = x_ref[:, pl.ds(col, tn)] + jnp.dot(
            act_scr[...], wo_ref[...], preferred_element_type=F32)


def _conv_ffn(x, gain, w_in, w_conv, w_out, *, tm, tf, tn, seq):
    m, d = x.shape
    f = w_out.shape[0]
    assert w_conv.shape[0] == 3 and f % tf == 0 and d % tn == 0 and tm % (FFN_ROW_CHUNKS * SUBLANES) == 0
    nj = f // tf
    kern = functools.partial(_conv_ffn_kernel, tm=tm, seq=seq, nj=nj, tn=tn)
    proj = lambda j: jnp.minimum(j, nj - 1)
    actv = lambda j: jnp.clip(j - 1, 0, nj - 1)
    outc = lambda j: jnp.maximum(j - nj, 0)
    return pl.pallas_call(
        kern,
        grid=(m // tm, nj + d // tn),
        in_specs=[
            pl.BlockSpec((tm, d), lambda i, j: (i, 0)),
            pl.BlockSpec((1, d), lambda i, j: (0, 0)),
            pl.BlockSpec((d, tf), lambda i, j: (0, proj(j))),
            pl.BlockSpec((d, tf), lambda i, j: (0, nj + proj(j))),
            pl.BlockSpec((3, tf), lambda i, j: (0, actv(j))),
            pl.BlockSpec((f, tn), lambda i, j: (0, outc(j))),
        ],
        out_specs=pl.BlockSpec((tm, tn), lambda i, j: (i, outc(j))),
        out_shape=jax.ShapeDtypeStruct((m, d), F32),
        scratch_shapes=[
            pltpu.VMEM((tm, d), BF16),
            pltpu.VMEM((tm, f), BF16),
            pltpu.VMEM((2, SUBLANES + tm, tf), F32),
            pltpu.VMEM((2, tm, tf), F32),
            pltpu.VMEM((nj, SUBLANES, tf), F32),
        ],
        compiler_params=_params(2),
        name="conv_ffn",
    )(x, gain, w_in, w_in, w_conv, w_out)


def _qkv_kernel(x_ref, gkv_ref, gq_ref, wk_ref, wv_ref, wq_ref, kng_ref, qng_ref,
                qt_ref, ka_ref, vt_ref, km_ref, hkv_scr, hq_scr, *, tm, seq, hps, nbp):
    i = pl.program_id(0)

    @pl.when(pl.program_id(1) == 0)
    def _():
        xn = _rms_normalize(x_ref[...])
        hkv_scr[...] = (xn * gkv_ref[...]).astype(BF16)
        hq_scr[...] = (xn * gq_ref[...]).astype(BF16)

    hkv = hkv_scr[...]
    kk = jnp.dot(hkv, wk_ref[...], preferred_element_type=F32)
    vv = jnp.dot(hkv, wv_ref[...], preferred_element_type=F32)
    qq = jnp.dot(hq_scr[...], wq_ref[...], preferred_element_type=F32)

    nblk = tm // MOBA_BLOCK
    blk0 = ((i * tm) % seq) // MOBA_BLOCK
    shift = MOBA_BLOCK.bit_length() - 1
    row_blk = blk0 + lax.shift_right_logical(
        lax.broadcasted_iota(jnp.int32, (tm, LANES), 0), jnp.full((tm, LANES), shift, jnp.int32))
    lane = lax.broadcasted_iota(jnp.int32, (tm, LANES), 1)
    onehot = (lane == row_blk).astype(BF16)
    pos = ((i * tm) % seq + lax.broadcasted_iota(jnp.int32, (tm, LANES), 0)).astype(F32)
    n_heads = x_ref.shape[1] // HEAD_DIM
    ones_row = (lax.broadcasted_iota(jnp.int32, (VT_PAD, MOBA_BLOCK), 0) == 0).astype(BF16)

    for hh in range(hps):
        sl = slice(hh * HEAD_DIM, (hh + 1) * HEAD_DIM)
        kn = _rms_normalize(kk[:, sl]) * kng_ref[...]
        qt_ref[0, hh] = (_rms_normalize(qq[:, sl]) * qng_ref[...]).T
        ka_ref[0, hh, :, 0:HEAD_DIM] = kn.astype(BF16)
        hv = (jnp.zeros((tm, LANES), jnp.int32) + (pl.program_id(1) * hps + hh + 1)).astype(F32)
        a = jnp.exp2(hv * (-8.0 / n_heads)) * pos
        a_hi = a.astype(BF16)
        r1 = a - a_hi.astype(F32)
        a_mid = r1.astype(BF16)
        a_lo = (r1 - a_mid.astype(F32)).astype(BF16)
        ka_ref[0, hh, :, HEAD_DIM:HEAD_DIM + LANES] = jnp.where(
            lane == nbp, a_hi, jnp.where(lane == nbp + 1, a_mid,
                                         jnp.where(lane == nbp + 2, a_lo, onehot)))
        vt = vv[:, sl].T.astype(BF16)
        for bb in range(nblk):
            cols = slice(bb * MOBA_BLOCK, (bb + 1) * MOBA_BLOCK)
            vt_ref[0, hh, bb, 0:HEAD_DIM, :] = vt[:, cols]
            vt_ref[0, hh, bb, HEAD_DIM:HEAD_DIM + VT_PAD, :] = ones_row
            km_ref[0, 0, hh, bb:bb + 1, :] = jnp.mean(kn[cols, :], axis=0, keepdims=True)


def _qkv(x, g_kv, g_q, w_kv, w_q, k_norm_g, q_norm_g, *, tm, tn, batch, seq):
    m, d = x.shape
    assert tn % HEAD_DIM == 0 and tm % MOBA_BLOCK == 0 and seq % tm == 0
    nj = d // tn
    hps = tn // HEAD_DIM
    n_heads = d // HEAD_DIM
    tps = seq // tm
    nblk = tm // MOBA_BLOCK
    nbp = _padded_blocks(seq)
    assert nbp + ALIBI_TERMS <= LANES
    kern = functools.partial(_qkv_kernel, tm=tm, seq=seq, hps=hps, nbp=nbp)
    row_d = pl.BlockSpec((1, d), lambda i, j: (0, 0))
    row_h = pl.BlockSpec((1, HEAD_DIM), lambda i, j: (0, 0))
    return pl.pallas_call(
        kern,
        grid=(m // tm, nj),
        in_specs=[
            pl.BlockSpec((tm, d), lambda i, j: (i, 0)),
            row_d, row_d,
            pl.BlockSpec((d, tn), lambda i, j: (0, j)),
            pl.BlockSpec((d, tn), lambda i, j: (0, nj + j)),
            pl.BlockSpec((d, tn), lambda i, j: (0, j)),
            row_h, row_h,
        ],
        out_specs=[
            pl.BlockSpec((1, hps, HEAD_DIM, tm), lambda i, j: (i // tps, j, 0, i % tps)),
            pl.BlockSpec((1, hps, tm, HEAD_DIM + LANES), lambda i, j: (i // tps, j, i % tps, 0)),
            pl.BlockSpec((1, hps, nblk, HEAD_DIM + VT_PAD, MOBA_BLOCK),
                         lambda i, j: (i // tps, j, i % tps, 0, 0)),
            pl.BlockSpec((1, 1, hps, nblk, HEAD_DIM), lambda i, j: (i // tps, i % tps, j, 0, 0)),
        ],
        out_shape=[
            jax.ShapeDtypeStruct((batch, n_heads, HEAD_DIM, seq), F32),
            jax.ShapeDtypeStruct((batch, n_heads, seq, HEAD_DIM + LANES), BF16),
            jax.ShapeDtypeStruct((batch, n_heads, seq // MOBA_BLOCK, HEAD_DIM + VT_PAD, MOBA_BLOCK),
                                 BF16),
            jax.ShapeDtypeStruct((batch, tps, n_heads, nblk, HEAD_DIM), F32),
        ],
        scratch_shapes=[pltpu.VMEM((tm, d), BF16), pltpu.VMEM((tm, d), BF16)],
        compiler_params=_params(2),
        name="qkv",
    )(x, g_kv, g_q, w_kv, w_kv, w_q, k_norm_g, q_norm_g)


SELECT_ROWS = 1024
MOBA_HEADS = 4


def _select_kernel(qt_ref, km_ref, qat_ref):
    rows = qt_ref.shape[3]
    nbp = km_ref.shape[2]
    qt = qt_ref[0, 0]
    gate = jnp.dot(km_ref[0, 0], qt, precision=lax.Precision.HIGHEST,
                   preferred_element_type=F32)
    blk = lax.broadcasted_iota(jnp.int32, gate.shape, 0)
    blk_f = blk.astype(F32)
    shift = MOBA_BLOCK.bit_length() - 1
    cur = pl.program_id(2) * (rows // MOBA_BLOCK) + lax.shift_right_logical(
        lax.broadcasted_iota(jnp.int32, gate.shape, 1), jnp.full(gate.shape, shift, jnp.int32))
    past = blk < cur
    work = jnp.where(past, gate, NEG_INF)
    sel = jnp.zeros(gate.shape, jnp.bool_)
    for _ in range(TOP_K):
        mx = jnp.max(work, axis=0, keepdims=True)
        idx = jnp.min(jnp.where(work == mx, blk_f, float(LANES)), axis=0, keepdims=True)
        hit = blk_f == idx
        sel = jnp.logical_or(sel, hit)
        work = jnp.where(hit, -jnp.inf, work)
    allowed = jnp.logical_or(jnp.logical_and(sel, past), blk == cur)
    selbias = jnp.where(allowed, 0.0, NEG_INF)
    qat_ref[0, 0, 0:HEAD_DIM, :] = (qt * HEAD_DIM ** -0.5).astype(BF16)
    qat_ref[0, 0, HEAD_DIM:HEAD_DIM + nbp, :] = selbias.astype(BF16)
    rest = LANES - nbp
    ones_rows = lax.broadcasted_iota(jnp.int32, (rest, rows), 0) < ALIBI_TERMS
    qat_ref[0, 0, HEAD_DIM + nbp:HEAD_DIM + LANES, :] = ones_rows.astype(BF16)


def _select(qt, km, *, seq):
    batch, n_heads, _, _ = qt.shape
    nbp = km.shape[2]
    rows = min(SELECT_ROWS, seq)
    assert seq % rows == 0 and rows % MOBA_BLOCK == 0 and nbp % 16 == 0 and nbp <= LANES
    return pl.pallas_call(
        _select_kernel,
        grid=(batch, n_heads, seq // rows),
        in_specs=[
            pl.BlockSpec((1, 1, HEAD_DIM, rows), lambda b, h, r: (b, h, 0, r)),
            pl.BlockSpec((1, 1, nbp, HEAD_DIM), lambda b, h, r: (b, h, 0, 0)),
        ],
        out_specs=pl.BlockSpec((1, 1, HEAD_DIM + LANES, rows), lambda b, h, r: (b, h, 0, r)),
        out_shape=jax.ShapeDtypeStruct((batch, n_heads, HEAD_DIM + LANES, seq), BF16),
        compiler_params=_params(3),
        name="select",
    )(qt, km)


def _moba_kernel(qat_ref, ka_ref, vt_ref, o_ref, acc_scr, m_scr, *, n_heads, hb):
    hg = pl.program_id(1)
    i = pl.program_id(2)
    tq = MOBA_BLOCK
    heads = range(hb)

    t_q = (i * tq + lax.broadcasted_iota(jnp.int32, (1, tq), 1)).astype(F32)
    b = []
    for hh in heads:
        hv = (jnp.zeros((1, tq), jnp.int32) + (hg * hb + hh + 1)).astype(F32)
        b.append(jnp.exp2(hv * (-8.0 / n_heads)) * t_q)

    def scores(n, hh):
        start = pl.multiple_of(n * tq, tq)
        return jnp.dot(ka_ref[0, hh, pl.ds(start, tq), :], qat_ref[0, hh],
                       preferred_element_type=F32)

    def key_max(s):
        return jnp.max(s, axis=0, keepdims=True)

    odd = i % 2
    near = jnp.maximum(i - 1, 0)
    near_bias = jnp.where(odd == 1, 0.0, NEG_INF)
    causal = (lax.broadcasted_iota(jnp.int32, (tq, tq), 1)
              >= lax.broadcasted_iota(jnp.int32, (tq, tq), 0))
    s0 = [jnp.where(causal, scores(i, hh), NEG_INF) for hh in heads]
    s1 = [scores(near, hh) + near_bias for hh in heads]
    m0 = [jnp.maximum(key_max(s0[hh]), key_max(s1[hh])) - b[hh] for hh in heads]
    p0 = [jnp.exp(s0[hh] - (m0[hh] + b[hh])).astype(BF16) for hh in heads]
    p1 = [jnp.exp(s1[hh] - (m0[hh] + b[hh])).astype(BF16) for hh in heads]
    for hh in heads:
        acc_scr[hh] = (jnp.dot(vt_ref[0, hh, i], p0[hh], preferred_element_type=F32)
                       + jnp.dot(vt_ref[0, hh, near], p1[hh], preferred_element_type=F32))
        m_scr[hh] = m0[hh]

    rest = i - odd

    def pair(t, carry):
        na = rest - 1 - 2 * t
        nb = na - 1
        sa = [scores(na, hh) for hh in heads]
        sb = [scores(nb, hh) for hh in heads]
        m_prev = [m_scr[hh] for hh in heads]
        m_new = [jnp.maximum(m_prev[hh], jnp.maximum(key_max(sa[hh]), key_max(sb[hh])) - b[hh])
                 for hh in heads]
        alpha = [jnp.exp(m_prev[hh] - m_new[hh]) for hh in heads]
        pa = [jnp.exp(sa[hh] - (m_new[hh] + b[hh])).astype(BF16) for hh in heads]
        pb = [jnp.exp(sb[hh] - (m_new[hh] + b[hh])).astype(BF16) for hh in heads]
        for hh in heads:
            acc_scr[hh] = (alpha[hh] * acc_scr[hh]
                           + jnp.dot(vt_ref[0, hh, na], pa[hh], preferred_element_type=F32)
                           + jnp.dot(vt_ref[0, hh, nb], pb[hh], preferred_element_type=F32))
            m_scr[hh] = m_new[hh]
        return carry

    lax.fori_loop(0, rest // 2, pair, 0)
    for hh in heads:
        acc = acc_scr[hh]
        ot = acc[0:HEAD_DIM, :] / acc[HEAD_DIM:HEAD_DIM + 1, :]
        o_ref[:, hh * HEAD_DIM:(hh + 1) * HEAD_DIM] = ot.T.astype(o_ref.dtype)


def _moba(qat, ka, vt, *, seq):
    batch, n_heads, wide, _ = qat.shape
    d = n_heads * HEAD_DIM
    nb = seq // MOBA_BLOCK
    hb = MOBA_HEADS if n_heads % MOBA_HEADS == 0 else 1
    tq = MOBA_BLOCK
    kern = functools.partial(_moba_kernel, n_heads=n_heads, hb=hb)
    return pl.pallas_call(
        kern,
        grid=(batch, n_heads // hb, nb),
        in_specs=[
            pl.BlockSpec((1, hb, wide, tq), lambda b, g, i: (b, g, 0, i)),
            pl.BlockSpec((1, hb, seq, wide), lambda b, g, i: (b, g, 0, 0)),
            pl.BlockSpec((1, hb, nb, HEAD_DIM + VT_PAD, tq), lambda b, g, i: (b, g, 0, 0, 0)),
        ],
        out_specs=pl.BlockSpec((tq, hb * HEAD_DIM), lambda b, g, i: (b * nb + i, g)),
        out_shape=jax.ShapeDtypeStruct((batch * seq, d), BF16),
        scratch_shapes=[
            pltpu.VMEM((hb, HEAD_DIM + VT_PAD, tq), F32),
            pltpu.VMEM((hb, 1, tq), F32),
        ],
        compiler_params=_params(3),
        name="moba",
    )(qat, ka, vt)


def _wo_residual_kernel(o_ref, w_ref, x_ref, out_ref):
    out_ref[...] = x_ref[...] + jnp.dot(o_ref[...], w_ref[...], preferred_element_type=F32)


def _wo_residual(o, w, x, *, tm, tn):
    m, d = x.shape
    return pl.pallas_call(
        _wo_residual_kernel,
        grid=(m // tm, d // tn),
        in_specs=[
            pl.BlockSpec((tm, d), lambda i, j: (i, 0)),
            pl.BlockSpec((d, tn), lambda i, j: (0, j)),
            pl.BlockSpec((tm, tn), lambda i, j: (i, j)),
        ],
        out_specs=pl.BlockSpec((tm, tn), lambda i, j: (i, j)),
        out_shape=jax.ShapeDtypeStruct((m, d), F32),
        compiler_params=_params(2),
        name="wo_residual",
    )(o, w, x)


def _tiles(seq, d, f):
    def pick(n, cands):
        for c in cands:
            if n % c == 0:
                return c
        raise ValueError(f"no tile for {n}")
    return dict(
        tm=pick(seq, (512, 256)),
        tm_conv=pick(seq, (256,)),
        tn=pick(d, (512, 256, 128)),
        tf=pick(f, (512, 256, 128)),
    )


def kernel(x, conv_norm_g, conv_w_pw1, conv_b_pw1, conv_w_dw, conv_b_dw, conv_ln_g, conv_ln_b,
           conv_w_pw2, conv_b_pw2, kv_norm_g, w_kv, k_norm_g, attn_norm_g, w_q, q_norm_g, w_o,
           ffn_norm_g, ffn_w_in, ffn_w_conv, ffn_w_out):
    batch, seq, d = x.shape
    depth = ffn_w_in.shape[0]
    n_a = conv_w_pw1.shape[0]
    f = ffn_w_out.shape[1]
    t = _tiles(seq, d, f)
    n_heads = d // HEAD_DIM
    nb = seq // MOBA_BLOCK
    row = lambda v: v.reshape(1, -1)

    xf = x.reshape(batch * seq, d)
    kv = None
    for layer in range(depth):
        if layer < n_a:
            a = layer
            u = _pw1_glu(xf, row(conv_norm_g[a]), conv_w_pw1[a].astype(BF16), row(conv_b_pw1[a]),
                         tm=t["tm"], tn=t["tn"])
            xf = _conv_ln_pw2(u, xf, conv_w_dw[a], row(conv_b_dw[a]), row(conv_ln_g[a]),
                              row(conv_ln_b[a]), conv_w_pw2[a].astype(BF16), row(conv_b_pw2[a]),
                              tm=t["tm_conv"], seq=seq)
        else:
            b = layer - n_a
            qt, ka, vt, km = _qkv(xf, row(kv_norm_g), row(attn_norm_g[b]), w_kv.astype(BF16),
                                  w_q[b].astype(BF16), row(k_norm_g), row(q_norm_g[b]),
                                  tm=t["tm"], tn=t["tn"], batch=batch, seq=seq)
            if kv is None:
                km = km.transpose(0, 2, 1, 3, 4).reshape(batch, n_heads, nb, HEAD_DIM)
                km = jnp.pad(km, ((0, 0), (0, 0), (0, -nb % BF16_SUBLANES), (0, 0)))
                kv = (ka, vt, km)
            ka, vt, km = kv
            qat = _select(qt, km, seq=seq)
            o = _moba(qat, ka, vt, seq=seq)
            xf = _wo_residual(o, w_o[b].astype(BF16), xf, tm=t["tm"], tn=t["tn"])
        xf = _conv_ffn(xf, row(ffn_norm_g[layer]), ffn_w_in[layer].astype(BF16), ffn_w_conv[layer],
                       ffn_w_out[layer].astype(BF16), tm=t["tm"], tf=t["tf"], tn=t["tn"], seq=seq)
    return xf.reshape(batch, seq, d)
```

```python
import functools

import jax
import jax.numpy as jnp
from jax import lax
from jax.experimental import pallas as pl
from jax.experimental.pallas import tpu as pltpu

EPS = 1e-6
NEG_INF = -1e30
HEAD_DIM = 128
MOBA_BLOCK = 256
TOP_K = 3
LANES = 128
SUBLANES = 8
BF16_SUBLANES = 16
VT_PAD = BF16_SUBLANES
ALIBI_TERMS = 3
ROW_CHUNKS = 4


def _padded_blocks(seq):
    nb = seq // MOBA_BLOCK
    return -(-nb // BF16_SUBLANES) * BF16_SUBLANES
VMEM_LIMIT_BYTES = 56 * 1024 * 1024

F32 = jnp.float32
BF16 = jnp.bfloat16


def _params(n_axes):
    return pltpu.CompilerParams(dimension_semantics=("arbitrary",) * n_axes,
                                vmem_limit_bytes=VMEM_LIMIT_BYTES)


def _rms_normalize(x):
    return x * lax.rsqrt(jnp.mean(x * x, axis=-1, keepdims=True) + EPS)


def _silu(x):
    return x * jax.nn.sigmoid(x)


def _pw1_glu_kernel(x_ref, g_ref, wa_ref, wg_ref, ba_ref, bg_ref, u_ref, h_scr):
    @pl.when(pl.program_id(1) == 0)
    def _():
        h_scr[...] = (_rms_normalize(x_ref[...]) * g_ref[...]).astype(BF16)

    rw = x_ref.shape[0] // ROW_CHUNKS
    for c in range(ROW_CHUNKS):
        rows = slice(c * rw, (c + 1) * rw)
        h = h_scr[rows, :]
        a = jnp.dot(h, wa_ref[0], preferred_element_type=F32) + ba_ref[...]
        gate = jnp.dot(h, wg_ref[0], preferred_element_type=F32) + bg_ref[...]
        u_ref[rows, :] = a * jax.nn.sigmoid(gate)


def _pw1_glu(x, gain, w1, b1, *, tm, tn):
    m, d = x.shape
    nj = d // tn
    return pl.pallas_call(
        _pw1_glu_kernel,
        grid=(m // tm, nj),
        in_specs=[
            pl.BlockSpec((tm, d), lambda i, j: (i, 0)),
            pl.BlockSpec((1, d), lambda i, j: (0, 0)),
            pl.BlockSpec((1, d, tn), lambda i, j: (j, 0, 0)),
            pl.BlockSpec((1, d, tn), lambda i, j: (nj + j, 0, 0)),
            pl.BlockSpec((1, tn), lambda i, j: (0, j)),
            pl.BlockSpec((1, tn), lambda i, j: (0, nj + j)),
        ],
        out_specs=pl.BlockSpec((tm, tn), lambda i, j: (i, j)),
        out_shape=jax.ShapeDtypeStruct((m, d), F32),
        scratch_shapes=[pltpu.VMEM((tm, d), BF16)],
        compiler_params=_params(2),
        name="pw1_glu",
    )(x, gain, w1, w1, b1, b1)


CONV_HALO = 32
CONV_ROWS = 64


def _conv_ln_pw2_kernel(u_ref, uh_ref, x_ref, wdw_ref, bdw_ref, lng_ref, lnb_ref, w2_ref, b2_ref,
                        o_ref, ub_scr, y_scr, *, tm, seq, width):
    i = pl.program_id(0)
    d = u_ref.shape[1]
    first = (i * tm) % seq == 0

    @pl.when(first)
    def _():
        ub_scr[0:CONV_HALO, :] = jnp.zeros((CONV_HALO, d), F32)

    @pl.when(jnp.logical_not(first))
    def _():
        ub_scr[0:CONV_HALO, :] = uh_ref[...]

    ub_scr[CONV_HALO:CONV_HALO + tm, :] = u_ref[...]

    off0 = CONV_HALO - (width - 1)
    max_off = off0 + width - 1
    n_a = max_off // SUBLANES + 1
    z_rows = CONV_ROWS + SUBLANES
    win_rows = z_rows + (n_a - 1) * SUBLANES

    def strip(c, carry):
        col = pl.multiple_of(c * LANES, LANES)
        for r in range(tm // CONV_ROWS):
            base = r * CONV_ROWS
            rows = min(win_rows, CONV_HALO + tm - base)
            win = ub_scr[pl.ds(base, rows), pl.ds(col, LANES)]
            y = None
            for b in range(SUBLANES):
                z = None
                for a in range(n_a):
                    k = SUBLANES * a + b - off0
                    if 0 <= k < width:
                        lo = SUBLANES * a
                        hi = min(lo + z_rows, rows)
                        term = wdw_ref[pl.ds(k, 1), pl.ds(col, LANES)] * win[lo:hi, :]
                        if hi - lo < z_rows:
                            term = jnp.concatenate(
                                [term, jnp.zeros((z_rows - (hi - lo), LANES), F32)], axis=0)
                        z = term if z is None else z + term
                part = z[b:b + CONV_ROWS, :]
                y = part if y is None else y + part
            y_scr[pl.ds(base, CONV_ROWS), pl.ds(col, LANES)] = y
        return carry

    lax.fori_loop(0, d // LANES, strip, 0)

    y = y_scr[...] + bdw_ref[...]
    mu = jnp.mean(y, axis=-1, keepdims=True)
    yc = y - mu
    var = jnp.mean(yc * yc, axis=-1, keepdims=True)
    yn = yc * lax.rsqrt(var + EPS) * lng_ref[...] + lnb_ref[...]
    act = _silu(yn).astype(BF16)
    o_ref[...] = x_ref[...] + (jnp.dot(act, w2_ref[...], preferred_element_type=F32) + b2_ref[...])


def _conv_ln_pw2(u, x, w_dw, b_dw, ln_g, ln_b, w2, b2, *, tm, seq):
    m, d = u.shape
    width = w_dw.shape[0]
    assert width - 1 <= CONV_HALO and tm % CONV_ROWS == 0 and tm % CONV_HALO == 0
    hb = tm // CONV_HALO
    kern = functools.partial(_conv_ln_pw2_kernel, tm=tm, seq=seq, width=width)
    row = pl.BlockSpec((1, d), lambda i: (0, 0))
    return pl.pallas_call(
        kern,
        grid=(m // tm,),
        in_specs=[
            pl.BlockSpec((tm, d), lambda i: (i, 0)),
            pl.BlockSpec((CONV_HALO, d), lambda i: (jnp.maximum(i * hb - 1, 0), 0)),
            pl.BlockSpec((tm, d), lambda i: (i, 0)),
            pl.BlockSpec((width, d), lambda i: (0, 0)),
            row, row, row,
            pl.BlockSpec((d, d), lambda i: (0, 0)),
            row,
        ],
        out_specs=pl.BlockSpec((tm, d), lambda i: (i, 0)),
        out_shape=jax.ShapeDtypeStruct((m, d), F32),
        scratch_shapes=[pltpu.VMEM((CONV_HALO + tm, d), F32), pltpu.VMEM((tm, d), F32)],
        compiler_params=_params(1),
        name="conv_ln_pw2",
    )(u, u, x, w_dw, b_dw, ln_g, ln_b, w2, b2)


FFN_ROW_CHUNKS = ROW_CHUNKS


def _conv_ffn_kernel(x_ref, g_ref, wg_ref, wu_ref, wc_ref, wo_ref, o_ref,
                     h_scr, act_scr, g_scr, u_scr, tail_scr, *, tm, seq, nj, tn):
    i = pl.program_id(0)
    j = pl.program_id(1)
    tf = wg_ref.shape[2]
    first = (i * tm) % seq == 0
    rw = tm // FFN_ROW_CHUNKS

    def project(slot, c):
        rows = slice(c * rw, (c + 1) * rw)
        h = h_scr[rows, :]
        g_scr[slot, SUBLANES + c * rw:SUBLANES + (c + 1) * rw, :] = jnp.dot(
            h, wg_ref[0], preferred_element_type=F32)
        u_scr[slot, rows, :] = jnp.dot(h, wu_ref[0], preferred_element_type=F32)

    def activate(jj, slot, c):
        if c == 0:
            g_scr[slot, 0:SUBLANES, :] = jnp.where(first, 0.0, tail_scr[jj])
        g = g_scr[slot, pl.ds(SUBLANES + c * rw, rw), :]
        g1 = g_scr[slot, pl.ds(SUBLANES - 1 + c * rw, rw), :]
        g2 = g_scr[slot, pl.ds(SUBLANES - 2 + c * rw, rw), :]
        if c == FFN_ROW_CHUNKS - 1:
            tail_scr[jj] = g_scr[slot, pl.ds(tm, SUBLANES), :]
        gc = wc_ref[2:3, :] * g + wc_ref[1:2, :] * g1 + wc_ref[0:1, :] * g2
        act = (_silu(gc) * u_scr[slot, c * rw:(c + 1) * rw, :]).astype(BF16)
        act_scr[c * rw:(c + 1) * rw, pl.ds(pl.multiple_of(jj * tf, tf), tf)] = act

    @pl.when(jnp.logical_and(i == 0, j == 0))
    def _():
        tail_scr[...] = jnp.zeros(tail_scr.shape, F32)

    @pl.when(j == 0)
    def _():
        h_scr[...] = (_rms_normalize(x_ref[...]) * g_ref[...]).astype(BF16)
        for c in range(FFN_ROW_CHUNKS):
            project(0, c)

    @pl.when(jnp.logical_and(j >= 1, j < nj))
    def _():
        slot = j % 2
        for c in range(FFN_ROW_CHUNKS):
            project(slot, c)
            activate(j - 1, 1 - slot, c)

    @pl.when(j == nj)
    def _():
        for c in range(FFN_ROW_CHUNKS):
            activate(nj - 1, (nj - 1) % 2, c)

    @pl.when(j >= nj)
    def _():
        col = pl.multiple_of((j - nj) * tn, tn)
        o_ref[...] = x_ref[:, pl.ds(col, tn)] + jnp.dot(
            act_scr[...], wo_ref[0], preferred_element_type=F32)


def _conv_ffn(x, gain, w_in, w_conv, w_out, *, tm, tf, tn, seq):
    m, d = x.shape
    f = w_out.shape[1]
    assert w_conv.shape[0] == 3 and f % tf == 0 and d % tn == 0 and tm % (FFN_ROW_CHUNKS * SUBLANES) == 0
    nj = f // tf
    kern = functools.partial(_conv_ffn_kernel, tm=tm, seq=seq, nj=nj, tn=tn)
    proj = lambda j: jnp.minimum(j, nj - 1)
    actv = lambda j: jnp.clip(j - 1, 0, nj - 1)
    outc = lambda j: jnp.maximum(j - nj, 0)
    return pl.pallas_call(
        kern,
        grid=(m // tm, nj + d // tn),
        in_specs=[
            pl.BlockSpec((tm, d), lambda i, j: (i, 0)),
            pl.BlockSpec((1, d), lambda i, j: (0, 0)),
            pl.BlockSpec((1, d, tf), lambda i, j: (proj(j), 0, 0)),
            pl.BlockSpec((1, d, tf), lambda i, j: (nj + proj(j), 0, 0)),
            pl.BlockSpec((3, tf), lambda i, j: (0, actv(j))),
            pl.BlockSpec((1, f, tn), lambda i, j: (outc(j), 0, 0)),
        ],
        out_specs=pl.BlockSpec((tm, tn), lambda i, j: (i, outc(j))),
        out_shape=jax.ShapeDtypeStruct((m, d), F32),
        scratch_shapes=[
            pltpu.VMEM((tm, d), BF16),
            pltpu.VMEM((tm, f), BF16),
            pltpu.VMEM((2, SUBLANES + tm, tf), F32),
            pltpu.VMEM((2, tm, tf), F32),
            pltpu.VMEM((nj, SUBLANES, tf), F32),
        ],
        compiler_params=_params(2),
        name="conv_ffn",
    )(x, gain, w_in, w_in, w_conv, w_out)


def _qkv_kernel(x_ref, gkv_ref, gq_ref, wk_ref, wv_ref, wq_ref, kng_ref, qng_ref,
                qt_ref, ka_ref, vt_ref, km_ref, hkv_scr, hq_scr, *, tm, seq, hps, nbp):
    i = pl.program_id(0)

    @pl.when(pl.program_id(1) == 0)
    def _():
        xn = _rms_normalize(x_ref[...])
        hkv_scr[...] = (xn * gkv_ref[...]).astype(BF16)
        hq_scr[...] = (xn * gq_ref[...]).astype(BF16)

    nblk = tm // MOBA_BLOCK
    n_heads = x_ref.shape[1] // HEAD_DIM
    shape = (MOBA_BLOCK, LANES)
    lane = lax.broadcasted_iota(jnp.int32, shape, 1)
    ones_row = (lax.broadcasted_iota(jnp.int32, (VT_PAD, MOBA_BLOCK), 0) == 0).astype(BF16)

    for bb in range(nblk):
        rows = slice(bb * MOBA_BLOCK, (bb + 1) * MOBA_BLOCK)
        hkv = hkv_scr[rows, :]
        kk = jnp.dot(hkv, wk_ref[0], preferred_element_type=F32)
        vv = jnp.dot(hkv, wv_ref[0], preferred_element_type=F32)
        qq = jnp.dot(hq_scr[rows, :], wq_ref[0], preferred_element_type=F32)

        row0 = (i * tm) % seq + bb * MOBA_BLOCK
        onehot = (lane == row0 // MOBA_BLOCK).astype(BF16)
        pos = (row0 + lax.broadcasted_iota(jnp.int32, shape, 0)).astype(F32)

        for hh in range(hps):
            sl = slice(hh * HEAD_DIM, (hh + 1) * HEAD_DIM)
            kn = _rms_normalize(kk[:, sl]) * kng_ref[...]
            qt_ref[0, hh, :, rows] = (_rms_normalize(qq[:, sl]) * qng_ref[...]).T
            ka_ref[0, hh, rows, 0:HEAD_DIM] = kn.astype(BF16)
            hv = (jnp.zeros(shape, jnp.int32) + (pl.program_id(1) * hps + hh + 1)).astype(F32)
            a = jnp.exp2(hv * (-8.0 / n_heads)) * pos
            a_hi = a.astype(BF16)
            r1 = a - a_hi.astype(F32)
            a_mid = r1.astype(BF16)
            a_lo = (r1 - a_mid.astype(F32)).astype(BF16)
            ka_ref[0, hh, rows, HEAD_DIM:HEAD_DIM + LANES] = jnp.where(
                lane == nbp, a_hi, jnp.where(lane == nbp + 1, a_mid,
                                             jnp.where(lane == nbp + 2, a_lo, onehot)))
            vt_ref[0, hh, bb, 0:HEAD_DIM, :] = vv[:, sl].T.astype(BF16)
            vt_ref[0, hh, bb, HEAD_DIM:HEAD_DIM + VT_PAD, :] = ones_row
            km_ref[0, 0, hh, bb:bb + 1, :] = jnp.mean(kn, axis=0, keepdims=True)


def _qkv(x, g_kv, g_q, w_kv, w_q, k_norm_g, q_norm_g, *, tm, tn, batch, seq):
    m, d = x.shape
    assert tn % HEAD_DIM == 0 and tm % MOBA_BLOCK == 0 and seq % tm == 0
    nj = d // tn
    hps = tn // HEAD_DIM
    n_heads = d // HEAD_DIM
    tps = seq // tm
    nblk = tm // MOBA_BLOCK
    nbp = _padded_blocks(seq)
    assert nbp + ALIBI_TERMS <= LANES
    kern = functools.partial(_qkv_kernel, tm=tm, seq=seq, hps=hps, nbp=nbp)
    row_d = pl.BlockSpec((1, d), lambda i, j: (0, 0))
    row_h = pl.BlockSpec((1, HEAD_DIM), lambda i, j: (0, 0))
    return pl.pallas_call(
        kern,
        grid=(m // tm, nj),
        in_specs=[
            pl.BlockSpec((tm, d), lambda i, j: (i, 0)),
            row_d, row_d,
            pl.BlockSpec((1, d, tn), lambda i, j: (j, 0, 0)),
            pl.BlockSpec((1, d, tn), lambda i, j: (nj + j, 0, 0)),
            pl.BlockSpec((1, d, tn), lambda i, j: (j, 0, 0)),
            row_h, row_h,
        ],
        out_specs=[
            pl.BlockSpec((1, hps, HEAD_DIM, tm), lambda i, j: (i // tps, j, 0, i % tps)),
            pl.BlockSpec((1, hps, tm, HEAD_DIM + LANES), lambda i, j: (i // tps, j, i % tps, 0)),
            pl.BlockSpec((1, hps, nblk, HEAD_DIM + VT_PAD, MOBA_BLOCK),
                         lambda i, j: (i // tps, j, i % tps, 0, 0)),
            pl.BlockSpec((1, 1, hps, nblk, HEAD_DIM), lambda i, j: (i // tps, i % tps, j, 0, 0)),
        ],
        out_shape=[
            jax.ShapeDtypeStruct((batch, n_heads, HEAD_DIM, seq), F32),
            jax.ShapeDtypeStruct((batch, n_heads, seq, HEAD_DIM + LANES), BF16),
            jax.ShapeDtypeStruct((batch, n_heads, seq // MOBA_BLOCK, HEAD_DIM + VT_PAD, MOBA_BLOCK),
                                 BF16),
            jax.ShapeDtypeStruct((batch, tps, n_heads, nblk, HEAD_DIM), F32),
        ],
        scratch_shapes=[pltpu.VMEM((tm, d), BF16), pltpu.VMEM((tm, d), BF16)],
        compiler_params=_params(2),
        name="qkv",
    )(x, g_kv, g_q, w_kv, w_kv, w_q, k_norm_g, q_norm_g)


SELECT_ROWS = 1024
MOBA_HEADS = 4


def _select_kernel(qt_ref, km_ref, qat_ref):
    rows = qt_ref.shape[3]
    nbp = km_ref.shape[2]
    qt = qt_ref[0, 0]
    gate = jnp.dot(km_ref[0, 0], qt, precision=lax.Precision.HIGHEST,
                   preferred_element_type=F32)
    blk = lax.broadcasted_iota(jnp.int32, gate.shape, 0)
    blk_f = blk.astype(F32)
    shift = MOBA_BLOCK.bit_length() - 1
    cur = pl.program_id(2) * (rows // MOBA_BLOCK) + lax.shift_right_logical(
        lax.broadcasted_iota(jnp.int32, gate.shape, 1), jnp.full(gate.shape, shift, jnp.int32))
    past = blk < cur
    work = jnp.where(past, gate, NEG_INF)
    sel = jnp.zeros(gate.shape, jnp.bool_)
    for _ in range(TOP_K):
        mx = jnp.max(work, axis=0, keepdims=True)
        idx = jnp.min(jnp.where(work == mx, blk_f, float(LANES)), axis=0, keepdims=True)
        hit = blk_f == idx
        sel = jnp.logical_or(sel, hit)
        work = jnp.where(hit, -jnp.inf, work)
    allowed = jnp.logical_or(jnp.logical_and(sel, past), blk == cur)
    selbias = jnp.where(allowed, 0.0, NEG_INF)
    qat_ref[0, 0, 0:HEAD_DIM, :] = (qt * HEAD_DIM ** -0.5).astype(BF16)
    qat_ref[0, 0, HEAD_DIM:HEAD_DIM + nbp, :] = selbias.astype(BF16)
    rest = LANES - nbp
    ones_rows = lax.broadcasted_iota(jnp.int32, (rest, rows), 0) < ALIBI_TERMS
    qat_ref[0, 0, HEAD_DIM + nbp:HEAD_DIM + LANES, :] = ones_rows.astype(BF16)


def _select(qt, km, *, seq):
    batch, n_heads, _, _ = qt.shape
    nbp = km.shape[2]
    rows = min(SELECT_ROWS, seq)
    assert seq % rows == 0 and rows % MOBA_BLOCK == 0 and nbp % 16 == 0 and nbp <= LANES
    return pl.pallas_call(
        _select_kernel,
        grid=(batch, n_heads, seq // rows),
        in_specs=[
            pl.BlockSpec((1, 1, HEAD_DIM, rows), lambda b, h, r: (b, h, 0, r)),
            pl.BlockSpec((1, 1, nbp, HEAD_DIM), lambda b, h, r: (b, h, 0, 0)),
        ],
        out_specs=pl.BlockSpec((1, 1, HEAD_DIM + LANES, rows), lambda b, h, r: (b, h, 0, r)),
        out_shape=jax.ShapeDtypeStruct((batch, n_heads, HEAD_DIM + LANES, seq), BF16),
        compiler_params=_params(3),
        name="select",
    )(qt, km)


def _moba_kernel(qat_ref, ka_ref, vt_ref, o_ref, acc_scr, m_scr, *, n_heads, hb):
    hg = pl.program_id(1)
    i = pl.program_id(2)
    tq = MOBA_BLOCK
    heads = range(hb)

    t_q = (i * tq + lax.broadcasted_iota(jnp.int32, (1, tq), 1)).astype(F32)
    b = []
    for hh in heads:
        hv = (jnp.zeros((1, tq), jnp.int32) + (hg * hb + hh + 1)).astype(F32)
        b.append(jnp.exp2(hv * (-8.0 / n_heads)) * t_q)

    def scores(n, hh):
        start = pl.multiple_of(n * tq, tq)
        return jnp.dot(ka_ref[0, hh, pl.ds(start, tq), :], qat_ref[0, hh],
                       preferred_element_type=F32)

    def key_max(s):
        return jnp.max(s, axis=0, keepdims=True)

    odd = i % 2
    near = jnp.maximum(i - 1, 0)
    near_bias = jnp.where(odd == 1, 0.0, NEG_INF)
    causal = (lax.broadcasted_iota(jnp.int32, (tq, tq), 1)
              >= lax.broadcasted_iota(jnp.int32, (tq, tq), 0))
    s0 = [jnp.where(causal, scores(i, hh), NEG_INF) for hh in heads]
    s1 = [scores(near, hh) + near_bias for hh in heads]
    m0 = [jnp.maximum(key_max(s0[hh]), key_max(s1[hh])) - b[hh] for hh in heads]
    p0 = [jnp.exp(s0[hh] - (m0[hh] + b[hh])).astype(BF16) for hh in heads]
    p1 = [jnp.exp(s1[hh] - (m0[hh] + b[hh])).astype(BF16) for hh in heads]
    for hh in heads:
        acc_scr[hh] = (jnp.dot(vt_ref[0, hh, i], p0[hh], preferred_element_type=F32)
                       + jnp.dot(vt_ref[0, hh, near], p1[hh], preferred_element_type=F32))
        m_scr[hh] = m0[hh]

    rest = i - odd

    def pair(t, carry):
        na = rest - 1 - 2 * t
        nb = na - 1
        sa = [scores(na, hh) for hh in heads]
        sb = [scores(nb, hh) for hh in heads]
        m_prev = [m_scr[hh] for hh in heads]
        m_new = [jnp.maximum(m_prev[hh], jnp.maximum(key_max(sa[hh]), key_max(sb[hh])) - b[hh])
                 for hh in heads]
        alpha = [jnp.exp(m_prev[hh] - m_new[hh]) for hh in heads]
        pa = [jnp.exp(sa[hh] - (m_new[hh] + b[hh])).astype(BF16) for hh in heads]
        pb = [jnp.exp(sb[hh] - (m_new[hh] + b[hh])).astype(BF16) for hh in heads]
        for hh in heads:
            acc_scr[hh] = (alpha[hh] * acc_scr[hh]
                           + jnp.dot(vt_ref[0, hh, na], pa[hh], preferred_element_type=F32)
                           + jnp.dot(vt_ref[0, hh, nb], pb[hh], preferred_element_type=F32))
            m_scr[hh] = m_new[hh]
        return carry

    lax.fori_loop(0, rest // 2, pair, 0)
    for hh in heads:
        acc = acc_scr[hh]
        ot = acc[0:HEAD_DIM, :] / acc[HEAD_DIM:HEAD_DIM + 1, :]
        o_ref[:, hh * HEAD_DIM:(hh + 1) * HEAD_DIM] = ot.T.astype(o_ref.dtype)


def _moba(qat, ka, vt, *, seq):
    batch, n_heads, wide, _ = qat.shape
    d = n_heads * HEAD_DIM
    nb = seq // MOBA_BLOCK
    hb = MOBA_HEADS if n_heads % MOBA_HEADS == 0 else 1
    tq = MOBA_BLOCK
    kern = functools.partial(_moba_kernel, n_heads=n_heads, hb=hb)
    return pl.pallas_call(
        kern,
        grid=(batch, n_heads // hb, nb),
        in_specs=[
            pl.BlockSpec((1, hb, wide, tq), lambda b, g, i: (b, g, 0, i)),
            pl.BlockSpec((1, hb, seq, wide), lambda b, g, i: (b, g, 0, 0)),
            pl.BlockSpec((1, hb, nb, HEAD_DIM + VT_PAD, tq), lambda b, g, i: (b, g, 0, 0, 0)),
        ],
        out_specs=pl.BlockSpec((tq, hb * HEAD_DIM), lambda b, g, i: (b * nb + i, g)),
        out_shape=jax.ShapeDtypeStruct((batch * seq, d), BF16),
        scratch_shapes=[
            pltpu.VMEM((hb, HEAD_DIM + VT_PAD, tq), F32),
            pltpu.VMEM((hb, 1, tq), F32),
        ],
        compiler_params=_params(3),
        name="moba",
    )(qat, ka, vt)


def _wo_residual_kernel(o_ref, w_ref, x_ref, out_ref):
    out_ref[...] = x_ref[...] + jnp.dot(o_ref[...], w_ref[0], preferred_element_type=F32)


def _wo_residual(o, w, x, *, tm, tn):
    m, d = x.shape
    return pl.pallas_call(
        _wo_residual_kernel,
        grid=(m // tm, d // tn),
        in_specs=[
            pl.BlockSpec((tm, d), lambda i, j: (i, 0)),
            pl.BlockSpec((1, d, tn), lambda i, j: (j, 0, 0)),
            pl.BlockSpec((tm, tn), lambda i, j: (i, j)),
        ],
        out_specs=pl.BlockSpec((tm, tn), lambda i, j: (i, j)),
        out_shape=jax.ShapeDtypeStruct((m, d), F32),
        compiler_params=_params(2),
        name="wo_residual",
    )(o, w, x)


def _tiles(seq, d, f):
    def pick(n, cands):
        for c in cands:
            if n % c == 0:
                return c
        raise ValueError(f"no tile for {n}")
    return dict(
        tm=pick(seq, (512, 256)),
        tm_conv=pick(seq, (256,)),
        tn=pick(d, (512, 256, 128)),
        tf=pick(f, (512, 256, 128)),
    )


def kernel(x, conv_norm_g, conv_w_pw1, conv_b_pw1, conv_w_dw, conv_b_dw, conv_ln_g, conv_ln_b,
           conv_w_pw2, conv_b_pw2, kv_norm_g, w_kv, k_norm_g, attn_norm_g, w_q, q_norm_g, w_o,
           ffn_norm_g, ffn_w_in, ffn_w_conv, ffn_w_out):
    batch, seq, d = x.shape
    depth = ffn_w_in.shape[0]
    n_a = conv_w_pw1.shape[0]
    f = ffn_w_out.shape[1]
    t = _tiles(seq, d, f)
    n_heads = d // HEAD_DIM
    nb = seq // MOBA_BLOCK
    row = lambda v: v.reshape(1, -1)

    def slabs(w, width):
        k, n = w.shape
        return w.astype(BF16).reshape(k, n // width, width).transpose(1, 0, 2)

    xf = x.reshape(batch * seq, d)
    kv = None
    for layer in range(depth):
        if layer < n_a:
            a = layer
            u = _pw1_glu(xf, row(conv_norm_g[a]), slabs(conv_w_pw1[a], t["tn"]), row(conv_b_pw1[a]),
                         tm=t["tm"], tn=t["tn"])
            xf = _conv_ln_pw2(u, xf, conv_w_dw[a], row(conv_b_dw[a]), row(conv_ln_g[a]),
                              row(conv_ln_b[a]), conv_w_pw2[a].astype(BF16), row(conv_b_pw2[a]),
                              tm=t["tm_conv"], seq=seq)
        else:
            b = layer - n_a
            qt, ka, vt, km = _qkv(xf, row(kv_norm_g), row(attn_norm_g[b]), slabs(w_kv, t["tn"]),
                                  slabs(w_q[b], t["tn"]), row(k_norm_g), row(q_norm_g[b]),
                                  tm=t["tm"], tn=t["tn"], batch=batch, seq=seq)
            if kv is None:
                km = km.transpose(0, 2, 1, 3, 4).reshape(batch, n_heads, nb, HEAD_DIM)
                km = jnp.pad(km, ((0, 0), (0, 0), (0, -nb % BF16_SUBLANES), (0, 0)))
                kv = (ka, vt, km)
            ka, vt, km = kv
            qat = _select(qt, km, seq=seq)
            o = _moba(qat, ka, vt, seq=seq)
            xf = _wo_residual(o, slabs(w_o[b], t["tn"]), xf, tm=t["tm"], tn=t["tn"])
        xf = _conv_ffn(xf, row(ffn_norm_g[layer]), slabs(ffn_w_in[layer], t["tf"]), ffn_w_conv[layer],
                       slabs(ffn_w_out[layer], t["tn"]), tm=t["tm"], tf=t["tf"], tn=t["tn"], seq=seq)
    return xf.reshape(batch, seq, d)
```

```python
import functools

import jax
import jax.numpy as jnp
from jax import lax
from jax.experimental import pallas as pl
from jax.experimental.pallas import tpu as pltpu

EPS = 1e-6
NEG_INF = -1e30
HEAD_DIM = 128
MOBA_BLOCK = 256
TOP_K = 3
LANES = 128
SUBLANES = 8
BF16_SUBLANES = 16
VT_PAD = BF16_SUBLANES
ALIBI_TERMS = 3
VMEM_LIMIT_BYTES = 56 * 1024 * 1024

F32 = jnp.float32
BF16 = jnp.bfloat16


def _padded_blocks(seq):
    nb = seq // MOBA_BLOCK
    return -(-nb // BF16_SUBLANES) * BF16_SUBLANES


def _params(n_axes):
    return pltpu.CompilerParams(dimension_semantics=("arbitrary",) * n_axes,
                                vmem_limit_bytes=VMEM_LIMIT_BYTES)


def _rms_normalize(x):
    return x * lax.rsqrt(jnp.mean(x * x, axis=-1, keepdims=True) + EPS)


def _silu(x):
    return x * jax.nn.sigmoid(x)


def _pw1_glu_kernel(x_ref, g_ref, wa_ref, wg_ref, ba_ref, bg_ref, u_ref, h_scr):
    @pl.when(pl.program_id(1) == 0)
    def _():
        h_scr[...] = (_rms_normalize(x_ref[...]) * g_ref[...]).astype(BF16)

    h = h_scr[...]
    a = jnp.dot(h, wa_ref[...], preferred_element_type=F32) + ba_ref[...]
    gate = jnp.dot(h, wg_ref[...], preferred_element_type=F32) + bg_ref[...]
    u_ref[...] = a * jax.nn.sigmoid(gate)


def _pw1_glu(x, gain, w1, b1, *, tm, tn):
    m, d = x.shape
    nj = d // tn
    return pl.pallas_call(
        _pw1_glu_kernel,
        grid=(m // tm, nj),
        in_specs=[
            pl.BlockSpec((tm, d), lambda i, j: (i, 0)),
            pl.BlockSpec((1, d), lambda i, j: (0, 0)),
            pl.BlockSpec((d, tn), lambda i, j: (0, j)),
            pl.BlockSpec((d, tn), lambda i, j: (0, nj + j)),
            pl.BlockSpec((1, tn), lambda i, j: (0, j)),
            pl.BlockSpec((1, tn), lambda i, j: (0, nj + j)),
        ],
        out_specs=pl.BlockSpec((tm, tn), lambda i, j: (i, j)),
        out_shape=jax.ShapeDtypeStruct((m, d), F32),
        scratch_shapes=[pltpu.VMEM((tm, d), BF16)],
        compiler_params=_params(2),
        name="pw1_glu",
    )(x, gain, w1, w1, b1, b1)


CONV_HALO = 32
CONV_ROWS = 64


def _conv_ln_pw2_kernel(u_ref, uh_ref, x_ref, wdw_ref, bdw_ref, lng_ref, lnb_ref, w2_ref, b2_ref,
                        o_ref, ub_scr, y_scr, *, tm, seq, width):
    i = pl.program_id(0)
    d = u_ref.shape[1]
    first = (i * tm) % seq == 0

    @pl.when(first)
    def _():
        ub_scr[0:CONV_HALO, :] = jnp.zeros((CONV_HALO, d), F32)

    @pl.when(jnp.logical_not(first))
    def _():
        ub_scr[0:CONV_HALO, :] = uh_ref[...]

    ub_scr[CONV_HALO:CONV_HALO + tm, :] = u_ref[...]

    off0 = CONV_HALO - (width - 1)
    max_off = off0 + width - 1
    n_a = max_off // SUBLANES + 1
    z_rows = CONV_ROWS + SUBLANES
    win_rows = z_rows + (n_a - 1) * SUBLANES

    def strip(c, carry):
        col = pl.multiple_of(c * LANES, LANES)
        for r in range(tm // CONV_ROWS):
            base = r * CONV_ROWS
            rows = min(win_rows, CONV_HALO + tm - base)
            win = ub_scr[pl.ds(base, rows), pl.ds(col, LANES)]
            y = None
            for b in range(SUBLANES):
                z = None
                for a in range(n_a):
                    k = SUBLANES * a + b - off0
                    if 0 <= k < width:
                        lo = SUBLANES * a
                        hi = min(lo + z_rows, rows)
                        term = wdw_ref[pl.ds(k, 1), pl.ds(col, LANES)] * win[lo:hi, :]
                        if hi - lo < z_rows:
                            term = jnp.concatenate(
                                [term, jnp.zeros((z_rows - (hi - lo), LANES), F32)], axis=0)
                        z = term if z is None else z + term
                part = z[b:b + CONV_ROWS, :]
                y = part if y is None else y + part
            y_scr[pl.ds(base, CONV_ROWS), pl.ds(col, LANES)] = y
        return carry

    lax.fori_loop(0, d // LANES, strip, 0)

    y = y_scr[...] + bdw_ref[...]
    mu = jnp.mean(y, axis=-1, keepdims=True)
    yc = y - mu
    var = jnp.mean(yc * yc, axis=-1, keepdims=True)
    yn = yc * lax.rsqrt(var + EPS) * lng_ref[...] + lnb_ref[...]
    act = _silu(yn).astype(BF16)
    o_ref[...] = x_ref[...] + (jnp.dot(act, w2_ref[...], preferred_element_type=F32) + b2_ref[...])


def _conv_ln_pw2(u, x, w_dw, b_dw, ln_g, ln_b, w2, b2, *, tm, seq):
    m, d = u.shape
    width = w_dw.shape[0]
    assert width - 1 <= CONV_HALO and tm % CONV_ROWS == 0 and tm % CONV_HALO == 0
    hb = tm // CONV_HALO
    kern = functools.partial(_conv_ln_pw2_kernel, tm=tm, seq=seq, width=width)
    row = pl.BlockSpec((1, d), lambda i: (0, 0))
    return pl.pallas_call(
        kern,
        grid=(m // tm,),
        in_specs=[
            pl.BlockSpec((tm, d), lambda i: (i, 0)),
            pl.BlockSpec((CONV_HALO, d), lambda i: (jnp.maximum(i * hb - 1, 0), 0)),
            pl.BlockSpec((tm, d), lambda i: (i, 0)),
            pl.BlockSpec((width, d), lambda i: (0, 0)),
            row, row, row,
            pl.BlockSpec((d, d), lambda i: (0, 0)),
            row,
        ],
        out_specs=pl.BlockSpec((tm, d), lambda i: (i, 0)),
        out_shape=jax.ShapeDtypeStruct((m, d), F32),
        scratch_shapes=[pltpu.VMEM((CONV_HALO + tm, d), F32), pltpu.VMEM((tm, d), F32)],
        compiler_params=_params(1),
        name="conv_ln_pw2",
    )(u, u, x, w_dw, b_dw, ln_g, ln_b, w2, b2)


FFN_CHUNKS = (2, 1)

def _conv_ffn_kernel(x_ref, g_ref, wg_ref, wu_ref, wc_ref, wo_ref, o_ref,
                     h_scr, act_scr, g_scr, u_scr, tail_scr, *, tm, seq, nj, tn, chunks):
    i = pl.program_id(0)
    j = pl.program_id(1)
    tf = wg_ref.shape[1]
    first = (i * tm) % seq == 0
    rw = tm // chunks

    def project(slot, c):
        rows = slice(c * rw, (c + 1) * rw)
        h = h_scr[rows, :]
        g_scr[slot, SUBLANES + c * rw:SUBLANES + (c + 1) * rw, :] = jnp.dot(
            h, wg_ref[...], preferred_element_type=F32)
        u_scr[slot, rows, :] = jnp.dot(h, wu_ref[...], preferred_element_type=F32)

    def activate(jj, slot, c):
        if c == 0:
            g_scr[slot, 0:SUBLANES, :] = jnp.where(first, 0.0, tail_scr[jj])
        g = g_scr[slot, pl.ds(SUBLANES + c * rw, rw), :]
        g1 = g_scr[slot, pl.ds(SUBLANES - 1 + c * rw, rw), :]
        g2 = g_scr[slot, pl.ds(SUBLANES - 2 + c * rw, rw), :]
        if c == chunks - 1:
            tail_scr[jj] = g_scr[slot, pl.ds(tm, SUBLANES), :]
        gc = wc_ref[2:3, :] * g + wc_ref[1:2, :] * g1 + wc_ref[0:1, :] * g2
        act = (_silu(gc) * u_scr[slot, c * rw:(c + 1) * rw, :]).astype(BF16)
        act_scr[c * rw:(c + 1) * rw, pl.ds(pl.multiple_of(jj * tf, tf), tf)] = act

    @pl.when(jnp.logical_and(i == 0, j == 0))
    def _():
        tail_scr[...] = jnp.zeros(tail_scr.shape, F32)

    @pl.when(j == 0)
    def _():
        h_scr[...] = (_rms_normalize(x_ref[...]) * g_ref[...]).astype(BF16)
        for c in range(chunks):
            project(0, c)

    @pl.when(jnp.logical_and(j >= 1, j < nj))
    def _():
        slot = j % 2
        for c in range(chunks):
            project(slot, c)
            activate(j - 1, 1 - slot, c)

    @pl.when(j == nj)
    def _():
        for c in range(chunks):
            activate(nj - 1, (nj - 1) % 2, c)

    @pl.when(j >= nj)
    def _():
        col = pl.multiple_of((j - nj) * tn, tn)
        o_ref[...] = x_ref[:, pl.ds(col, tn)] + jnp.dot(
            act_scr[...], wo_ref[...], preferred_element_type=F32)


def _conv_ffn(x, gain, w_in, w_conv, w_out, *, tm, tf, tn, seq, chunks):
    m, d = x.shape
    f = w_out.shape[0]
    assert w_conv.shape[0] == 3 and f % tf == 0 and d % tn == 0 and tm % (chunks * SUBLANES) == 0
    nj = f // tf
    kern = functools.partial(_conv_ffn_kernel, tm=tm, seq=seq, nj=nj, tn=tn, chunks=chunks)
    proj = lambda j: jnp.minimum(j, nj - 1)
    actv = lambda j: jnp.clip(j - 1, 0, nj - 1)
    outc = lambda j: jnp.maximum(j - nj, 0)
    return pl.pallas_call(
        kern,
        grid=(m // tm, nj + d // tn),
        in_specs=[
            pl.BlockSpec((tm, d), lambda i, j: (i, 0)),
            pl.BlockSpec((1, d), lambda i, j: (0, 0)),
            pl.BlockSpec((d, tf), lambda i, j: (0, proj(j))),
            pl.BlockSpec((d, tf), lambda i, j: (0, nj + proj(j))),
            pl.BlockSpec((3, tf), lambda i, j: (0, actv(j))),
            pl.BlockSpec((f, tn), lambda i, j: (0, outc(j))),
        ],
        out_specs=pl.BlockSpec((tm, tn), lambda i, j: (i, outc(j))),
        out_shape=jax.ShapeDtypeStruct((m, d), F32),
        scratch_shapes=[
            pltpu.VMEM((tm, d), BF16),
            pltpu.VMEM((tm, f), BF16),
            pltpu.VMEM((2, SUBLANES + tm, tf), F32),
            pltpu.VMEM((2, tm, tf), F32),
            pltpu.VMEM((nj, SUBLANES, tf), F32),
        ],
        compiler_params=_params(2),
        name="conv_ffn",
    )(x, gain, w_in, w_in, w_conv, w_out)


def _qkv_kernel(x_ref, gkv_ref, gq_ref, wk_ref, wv_ref, wq_ref, kng_ref, qng_ref,
                qt_ref, ka_ref, vt_ref, km_ref, hkv_scr, hq_scr, *, tm, seq, hps, nbp):
    i = pl.program_id(0)

    @pl.when(pl.program_id(1) == 0)
    def _():
        xn = _rms_normalize(x_ref[...])
        hkv_scr[...] = (xn * gkv_ref[...]).astype(BF16)
        hq_scr[...] = (xn * gq_ref[...]).astype(BF16)

    nblk = tm // MOBA_BLOCK
    n_heads = x_ref.shape[1] // HEAD_DIM
    shape = (MOBA_BLOCK, LANES)
    lane = lax.broadcasted_iota(jnp.int32, shape, 1)
    ones_row = (lax.broadcasted_iota(jnp.int32, (VT_PAD, MOBA_BLOCK), 0) == 0).astype(BF16)

    for bb in range(nblk):
        rows = slice(bb * MOBA_BLOCK, (bb + 1) * MOBA_BLOCK)
        hkv = hkv_scr[rows, :]
        kk = jnp.dot(hkv, wk_ref[...], preferred_element_type=F32)
        vv = jnp.dot(hkv, wv_ref[...], preferred_element_type=F32)
        qq = jnp.dot(hq_scr[rows, :], wq_ref[...], preferred_element_type=F32)

        row0 = (i * tm) % seq + bb * MOBA_BLOCK
        onehot = (lane == row0 // MOBA_BLOCK).astype(BF16)
        pos = (row0 + lax.broadcasted_iota(jnp.int32, shape, 0)).astype(F32)

        for hh in range(hps):
            sl = slice(hh * HEAD_DIM, (hh + 1) * HEAD_DIM)
            kn = _rms_normalize(kk[:, sl]) * kng_ref[...]
            qt_ref[0, hh, :, rows] = (_rms_normalize(qq[:, sl]) * qng_ref[...]).T
            ka_ref[0, hh, rows, 0:HEAD_DIM] = kn.astype(BF16)
            hv = (jnp.zeros(shape, jnp.int32) + (pl.program_id(1) * hps + hh + 1)).astype(F32)
            a = jnp.exp2(hv * (-8.0 / n_heads)) * pos
            a_hi = a.astype(BF16)
            r1 = a - a_hi.astype(F32)
            a_mid = r1.astype(BF16)
            a_lo = (r1 - a_mid.astype(F32)).astype(BF16)
            ka_ref[0, hh, rows, HEAD_DIM:HEAD_DIM + LANES] = jnp.where(
                lane == nbp, a_hi, jnp.where(lane == nbp + 1, a_mid,
                                             jnp.where(lane == nbp + 2, a_lo, onehot)))
            vt_ref[0, hh, bb, 0:HEAD_DIM, :] = vv[:, sl].T.astype(BF16)
            vt_ref[0, hh, bb, HEAD_DIM:HEAD_DIM + VT_PAD, :] = ones_row
            km_ref[0, 0, hh, bb:bb + 1, :] = jnp.mean(kn, axis=0, keepdims=True)


def _qkv(x, g_kv, g_q, w_kv, w_q, k_norm_g, q_norm_g, *, tm, tn, batch, seq):
    m, d = x.shape
    assert tn % HEAD_DIM == 0 and tm % MOBA_BLOCK == 0 and seq % tm == 0
    nj = d // tn
    hps = tn // HEAD_DIM
    n_heads = d // HEAD_DIM
    tps = seq // tm
    nblk = tm // MOBA_BLOCK
    nbp = _padded_blocks(seq)
    assert nbp + ALIBI_TERMS <= LANES
    kern = functools.partial(_qkv_kernel, tm=tm, seq=seq, hps=hps, nbp=nbp)
    row_d = pl.BlockSpec((1, d), lambda i, j: (0, 0))
    row_h = pl.BlockSpec((1, HEAD_DIM), lambda i, j: (0, 0))
    return pl.pallas_call(
        kern,
        grid=(m // tm, nj),
        in_specs=[
            pl.BlockSpec((tm, d), lambda i, j: (i, 0)),
            row_d, row_d,
            pl.BlockSpec((d, tn), lambda i, j: (0, j)),
            pl.BlockSpec((d, tn), lambda i, j: (0, nj + j)),
            pl.BlockSpec((d, tn), lambda i, j: (0, j)),
            row_h, row_h,
        ],
        out_specs=[
            pl.BlockSpec((1, hps, HEAD_DIM, tm), lambda i, j: (i // tps, j, 0, i % tps)),
            pl.BlockSpec((1, hps, tm, HEAD_DIM + LANES), lambda i, j: (i // tps, j, i % tps, 0)),
            pl.BlockSpec((1, hps, nblk, HEAD_DIM + VT_PAD, MOBA_BLOCK),
                         lambda i, j: (i // tps, j, i % tps, 0, 0)),
            pl.BlockSpec((1, 1, hps, nblk, HEAD_DIM), lambda i, j: (i // tps, i % tps, j, 0, 0)),
        ],
        out_shape=[
            jax.ShapeDtypeStruct((batch, n_heads, HEAD_DIM, seq), F32),
            jax.ShapeDtypeStruct((batch, n_heads, seq, HEAD_DIM + LANES), BF16),
            jax.ShapeDtypeStruct((batch, n_heads, seq // MOBA_BLOCK, HEAD_DIM + VT_PAD, MOBA_BLOCK),
                                 BF16),
            jax.ShapeDtypeStruct((batch, tps, n_heads, nblk, HEAD_DIM), F32),
        ],
        scratch_shapes=[pltpu.VMEM((tm, d), BF16), pltpu.VMEM((tm, d), BF16)],
        compiler_params=_params(2),
        name="qkv",
    )(x, g_kv, g_q, w_kv, w_kv, w_q, k_norm_g, q_norm_g)


MOBA_HEADS = 4


def _moba_kernel(qt_ref, km_ref, ka_ref, vt_ref, o_ref, qat_scr, acc_scr, m_scr, *, n_heads, hb):
    hg = pl.program_id(1)
    i = pl.program_id(2)
    tq = MOBA_BLOCK
    nbp = km_ref.shape[2]
    heads = range(hb)

    blk = lax.broadcasted_iota(jnp.int32, (nbp, tq), 0)
    blk_f = blk.astype(F32)
    past = blk < i
    rest_rows = LANES - nbp
    ones_rows = (lax.broadcasted_iota(jnp.int32, (rest_rows, tq), 0) < ALIBI_TERMS).astype(BF16)
    for hh in heads:
        qt = qt_ref[0, hh]
        gate = jnp.dot(km_ref[0, hh], qt, precision=lax.Precision.HIGHEST,
                       preferred_element_type=F32)
        work = jnp.where(past, gate, NEG_INF)
        sel = jnp.zeros(gate.shape, jnp.bool_)
        for _ in range(TOP_K):
            mx = jnp.max(work, axis=0, keepdims=True)
            idx = jnp.min(jnp.where(work == mx, blk_f, float(LANES)), axis=0, keepdims=True)
            hit = blk_f == idx
            sel = jnp.logical_or(sel, hit)
            work = jnp.where(hit, -jnp.inf, work)
        allowed = jnp.logical_or(jnp.logical_and(sel, past), blk == i)
        qat_scr[hh, 0:HEAD_DIM, :] = (qt * HEAD_DIM ** -0.5).astype(BF16)
        qat_scr[hh, HEAD_DIM:HEAD_DIM + nbp, :] = jnp.where(allowed, 0.0, NEG_INF).astype(BF16)
        qat_scr[hh, HEAD_DIM + nbp:HEAD_DIM + LANES, :] = ones_rows

    t_q = (i * tq + lax.broadcasted_iota(jnp.int32, (1, tq), 1)).astype(F32)
    b = []
    for hh in heads:
        hv = (jnp.zeros((1, tq), jnp.int32) + (hg * hb + hh + 1)).astype(F32)
        b.append(jnp.exp2(hv * (-8.0 / n_heads)) * t_q)

    def scores(n, hh):
        start = pl.multiple_of(n * tq, tq)
        return jnp.dot(ka_ref[0, hh, pl.ds(start, tq), :], qat_scr[hh],
                       preferred_element_type=F32)

    def key_max(s):
        return jnp.max(s, axis=0, keepdims=True)

    odd = i % 2
    near = jnp.maximum(i - 1, 0)
    near_bias = jnp.where(odd == 1, 0.0, NEG_INF)
    causal = (lax.broadcasted_iota(jnp.int32, (tq, tq), 1)
              >= lax.broadcasted_iota(jnp.int32, (tq, tq), 0))
    s0 = [jnp.where(causal, scores(i, hh), NEG_INF) for hh in heads]
    s1 = [scores(near, hh) + near_bias for hh in heads]
    m0 = [jnp.maximum(key_max(s0[hh]), key_max(s1[hh])) - b[hh] for hh in heads]
    p0 = [jnp.exp(s0[hh] - (m0[hh] + b[hh])).astype(BF16) for hh in heads]
    p1 = [jnp.exp(s1[hh] - (m0[hh] + b[hh])).astype(BF16) for hh in heads]
    for hh in heads:
        acc_scr[hh] = (jnp.dot(vt_ref[0, hh, i], p0[hh], preferred_element_type=F32)
                       + jnp.dot(vt_ref[0, hh, near], p1[hh], preferred_element_type=F32))
        m_scr[hh] = m0[hh]

    rest = i - odd

    def pair(t, carry):
        na = rest - 1 - 2 * t
        nb = na - 1
        sa = [scores(na, hh) for hh in heads]
        sb = [scores(nb, hh) for hh in heads]
        m_prev = [m_scr[hh] for hh in heads]
        m_new = [jnp.maximum(m_prev[hh], jnp.maximum(key_max(sa[hh]), key_max(sb[hh])) - b[hh])
                 for hh in heads]
        alpha = [jnp.exp(m_prev[hh] - m_new[hh]) for hh in heads]
        pa = [jnp.exp(sa[hh] - (m_new[hh] + b[hh])).astype(BF16) for hh in heads]
        pb = [jnp.exp(sb[hh] - (m_new[hh] + b[hh])).astype(BF16) for hh in heads]
        for hh in heads:
            acc_scr[hh] = (alpha[hh] * acc_scr[hh]
                           + jnp.dot(vt_ref[0, hh, na], pa[hh], preferred_element_type=F32)
                           + jnp.dot(vt_ref[0, hh, nb], pb[hh], preferred_element_type=F32))
            m_scr[hh] = m_new[hh]
        return carry

    lax.fori_loop(0, rest // 2, pair, 0)
    for hh in heads:
        acc = acc_scr[hh]
        ot = acc[0:HEAD_DIM, :] / acc[HEAD_DIM:HEAD_DIM + 1, :]
        o_ref[:, hh * HEAD_DIM:(hh + 1) * HEAD_DIM] = ot.T.astype(o_ref.dtype)


def _moba(qt, km, ka, vt, *, seq):
    batch, n_heads, _, _ = qt.shape
    d = n_heads * HEAD_DIM
    nb = seq // MOBA_BLOCK
    nbp = km.shape[2]
    assert nbp == _padded_blocks(seq) and nbp + ALIBI_TERMS <= LANES
    hb = MOBA_HEADS if n_heads % MOBA_HEADS == 0 else 1
    tq = MOBA_BLOCK
    wide = HEAD_DIM + LANES
    kern = functools.partial(_moba_kernel, n_heads=n_heads, hb=hb)
    return pl.pallas_call(
        kern,
        grid=(batch, n_heads // hb, nb),
        in_specs=[
            pl.BlockSpec((1, hb, HEAD_DIM, tq), lambda b, g, i: (b, g, 0, i)),
            pl.BlockSpec((1, hb, nbp, HEAD_DIM), lambda b, g, i: (b, g, 0, 0)),
            pl.BlockSpec((1, hb, seq, wide), lambda b, g, i: (b, g, 0, 0)),
            pl.BlockSpec((1, hb, nb, HEAD_DIM + VT_PAD, tq), lambda b, g, i: (b, g, 0, 0, 0)),
        ],
        out_specs=pl.BlockSpec((tq, hb * HEAD_DIM), lambda b, g, i: (b * nb + i, g)),
        out_shape=jax.ShapeDtypeStruct((batch * seq, d), BF16),
        scratch_shapes=[
            pltpu.VMEM((hb, wide, tq), BF16),
            pltpu.VMEM((hb, HEAD_DIM + VT_PAD, tq), F32),
            pltpu.VMEM((hb, 1, tq), F32),
        ],
        compiler_params=_params(3),
        name="moba",
    )(qt, km, ka, vt)


def _wo_residual_kernel(o_ref, w_ref, x_ref, out_ref):
    out_ref[...] = x_ref[...] + jnp.dot(o_ref[...], w_ref[...], preferred_element_type=F32)


def _wo_residual(o, w, x, *, tm, tn):
    m, d = x.shape
    return pl.pallas_call(
        _wo_residual_kernel,
        grid=(m // tm, d // tn),
        in_specs=[
            pl.BlockSpec((tm, d), lambda i, j: (i, 0)),
            pl.BlockSpec((d, tn), lambda i, j: (0, j)),
            pl.BlockSpec((tm, tn), lambda i, j: (i, j)),
        ],
        out_specs=pl.BlockSpec((tm, tn), lambda i, j: (i, j)),
        out_shape=jax.ShapeDtypeStruct((m, d), F32),
        compiler_params=_params(2),
        name="wo_residual",
    )(o, w, x)


def _tiles(seq, d, f):
    def pick(n, cands):
        for c in cands:
            if n % c == 0:
                return c
        raise ValueError(f"no tile for {n}")
    return dict(
        tm=pick(seq, (512, 256)),
        tm_conv=pick(seq, (256,)),
        tn=pick(d, (512, 256, 128)),
        tf=pick(f, (512, 256, 128)),
    )


def kernel(x, conv_norm_g, conv_w_pw1, conv_b_pw1, conv_w_dw, conv_b_dw, conv_ln_g, conv_ln_b,
           conv_w_pw2, conv_b_pw2, kv_norm_g, w_kv, k_norm_g, attn_norm_g, w_q, q_norm_g, w_o,
           ffn_norm_g, ffn_w_in, ffn_w_conv, ffn_w_out):
    batch, seq, d = x.shape
    depth = ffn_w_in.shape[0]
    n_a = conv_w_pw1.shape[0]
    f = ffn_w_out.shape[1]
    t = _tiles(seq, d, f)
    n_heads = d // HEAD_DIM
    nb = seq // MOBA_BLOCK
    row = lambda v: v.reshape(1, -1)

    xf = x.reshape(batch * seq, d)
    kv = None
    for layer in range(depth):
        if layer < n_a:
            a = layer
            u = _pw1_glu(xf, row(conv_norm_g[a]), conv_w_pw1[a].astype(BF16), row(conv_b_pw1[a]),
                         tm=t["tm"], tn=t["tn"])
            xf = _conv_ln_pw2(u, xf, conv_w_dw[a], row(conv_b_dw[a]), row(conv_ln_g[a]),
                              row(conv_ln_b[a]), conv_w_pw2[a].astype(BF16), row(conv_b_pw2[a]),
                              tm=t["tm_conv"], seq=seq)
        else:
            b = layer - n_a
            qt, ka, vt, km = _qkv(xf, row(kv_norm_g), row(attn_norm_g[b]), w_kv.astype(BF16),
                                  w_q[b].astype(BF16), row(k_norm_g), row(q_norm_g[b]),
                                  tm=t["tm"], tn=t["tn"], batch=batch, seq=seq)
            if kv is None:
                km = km.transpose(0, 2, 1, 3, 4).reshape(batch, n_heads, nb, HEAD_DIM)
                km = jnp.pad(km, ((0, 0), (0, 0), (0, -nb % BF16_SUBLANES), (0, 0)))
                kv = (ka, vt, km)
            ka, vt, km = kv
            o = _moba(qt, km, ka, vt, seq=seq)
            xf = _wo_residual(o, w_o[b].astype(BF16), xf, tm=t["tm"], tn=t["tn"])
        xf = _conv_ffn(xf, row(ffn_norm_g[layer]), ffn_w_in[layer].astype(BF16), ffn_w_conv[layer],
                       ffn_w_out[layer].astype(BF16), tm=t["tm"], tf=t["tf"], tn=t["tn"], seq=seq,
                       chunks=FFN_CHUNKS[layer % len(FFN_CHUNKS)])
    return xf.reshape(batch, seq, d)
```

```python
import functools

import jax
import jax.numpy as jnp
from jax import lax
from jax.experimental import pallas as pl
from jax.experimental.pallas import tpu as pltpu

EPS = 1e-6
NEG_INF = -1e30
HEAD_DIM = 128
MOBA_BLOCK = 256
TOP_K = 3
LANES = 128
SUBLANES = 8
BF16_SUBLANES = 16
VT_PAD = BF16_SUBLANES
ALIBI_TERMS = 3
VMEM_LIMIT_BYTES = 56 * 1024 * 1024

F32 = jnp.float32
BF16 = jnp.bfloat16


def _padded_blocks(seq):
    nb = seq // MOBA_BLOCK
    return -(-nb // BF16_SUBLANES) * BF16_SUBLANES


def _params(n_axes):
    return pltpu.CompilerParams(dimension_semantics=("arbitrary",) * n_axes,
                                vmem_limit_bytes=VMEM_LIMIT_BYTES)


def _rms_normalize(x):
    return x * lax.rsqrt(jnp.mean(x * x, axis=-1, keepdims=True) + EPS)


def _silu(x):
    return x * jax.nn.sigmoid(x)


def _pw1_glu_kernel(x_ref, g_ref, wa_ref, wg_ref, ba_ref, bg_ref, u_ref, h_scr):
    @pl.when(pl.program_id(1) == 0)
    def _():
        h_scr[...] = (_rms_normalize(x_ref[...]) * g_ref[...]).astype(BF16)

    h = h_scr[...]
    a = jnp.dot(h, wa_ref[...], preferred_element_type=F32) + ba_ref[...]
    gate = jnp.dot(h, wg_ref[...], preferred_element_type=F32) + bg_ref[...]
    u_ref[...] = a * jax.nn.sigmoid(gate)


def _pw1_glu(x, gain, w1, b1, *, tm, tn):
    m, d = x.shape
    nj = d // tn
    return pl.pallas_call(
        _pw1_glu_kernel,
        grid=(m // tm, nj),
        in_specs=[
            pl.BlockSpec((tm, d), lambda i, j: (i, 0)),
            pl.BlockSpec((1, d), lambda i, j: (0, 0)),
            pl.BlockSpec((d, tn), lambda i, j: (0, j)),
            pl.BlockSpec((d, tn), lambda i, j: (0, nj + j)),
            pl.BlockSpec((1, tn), lambda i, j: (0, j)),
            pl.BlockSpec((1, tn), lambda i, j: (0, nj + j)),
        ],
        out_specs=pl.BlockSpec((tm, tn), lambda i, j: (i, j)),
        out_shape=jax.ShapeDtypeStruct((m, d), F32),
        scratch_shapes=[pltpu.VMEM((tm, d), BF16)],
        compiler_params=_params(2),
        name="pw1_glu",
    )(x, gain, w1, w1, b1, b1)


CONV_HALO = 32
CONV_ROWS = 64


def _conv_ln_pw2_kernel(u_ref, uh_ref, x_ref, wdw_ref, bdw_ref, lng_ref, lnb_ref, w2_ref, b2_ref,
                        o_ref, ub_scr, y_scr, *, tm, seq, width):
    i = pl.program_id(0)
    d = u_ref.shape[1]
    first = (i * tm) % seq == 0

    @pl.when(first)
    def _():
        ub_scr[0:CONV_HALO, :] = jnp.zeros((CONV_HALO, d), F32)

    @pl.when(jnp.logical_not(first))
    def _():
        ub_scr[0:CONV_HALO, :] = uh_ref[...]

    ub_scr[CONV_HALO:CONV_HALO + tm, :] = u_ref[...]

    off0 = CONV_HALO - (width - 1)
    max_off = off0 + width - 1
    n_a = max_off // SUBLANES + 1
    z_rows = CONV_ROWS + SUBLANES
    win_rows = z_rows + (n_a - 1) * SUBLANES

    def strip(c, carry):
        col = pl.multiple_of(c * LANES, LANES)
        for r in range(tm // CONV_ROWS):
            base = r * CONV_ROWS
            rows = min(win_rows, CONV_HALO + tm - base)
            win = ub_scr[pl.ds(base, rows), pl.ds(col, LANES)]
            y = None
            for b in range(SUBLANES):
                z = None
                for a in range(n_a):
                    k = SUBLANES * a + b - off0
                    if 0 <= k < width:
                        lo = SUBLANES * a
                        hi = min(lo + z_rows, rows)
                        term = wdw_ref[pl.ds(k, 1), pl.ds(col, LANES)] * win[lo:hi, :]
                        if hi - lo < z_rows:
                            term = jnp.concatenate(
                                [term, jnp.zeros((z_rows - (hi - lo), LANES), F32)], axis=0)
                        z = term if z is None else z + term
                part = z[b:b + CONV_ROWS, :]
                y = part if y is None else y + part
            y_scr[pl.ds(base, CONV_ROWS), pl.ds(col, LANES)] = y
        return carry

    lax.fori_loop(0, d // LANES, strip, 0)

    y = y_scr[...] + bdw_ref[...]
    mu = jnp.mean(y, axis=-1, keepdims=True)
    yc = y - mu
    var = jnp.mean(yc * yc, axis=-1, keepdims=True)
    yn = yc * lax.rsqrt(var + EPS) * lng_ref[...] + lnb_ref[...]
    act = _silu(yn).astype(BF16)
    o_ref[...] = x_ref[...] + (jnp.dot(act, w2_ref[...], preferred_element_type=F32) + b2_ref[...])


def _conv_ln_pw2(u, x, w_dw, b_dw, ln_g, ln_b, w2, b2, *, tm, seq):
    m, d = u.shape
    width = w_dw.shape[0]
    assert width - 1 <= CONV_HALO and tm % CONV_ROWS == 0 and tm % CONV_HALO == 0
    hb = tm // CONV_HALO
    kern = functools.partial(_conv_ln_pw2_kernel, tm=tm, seq=seq, width=width)
    row = pl.BlockSpec((1, d), lambda i: (0, 0))
    return pl.pallas_call(
        kern,
        grid=(m // tm,),
        in_specs=[
            pl.BlockSpec((tm, d), lambda i: (i, 0)),
            pl.BlockSpec((CONV_HALO, d), lambda i: (jnp.maximum(i * hb - 1, 0), 0)),
            pl.BlockSpec((tm, d), lambda i: (i, 0)),
            pl.BlockSpec((width, d), lambda i: (0, 0)),
            row, row, row,
            pl.BlockSpec((d, d), lambda i: (0, 0)),
            row,
        ],
        out_specs=pl.BlockSpec((tm, d), lambda i: (i, 0)),
        out_shape=jax.ShapeDtypeStruct((m, d), F32),
        scratch_shapes=[pltpu.VMEM((CONV_HALO + tm, d), F32), pltpu.VMEM((tm, d), F32)],
        compiler_params=_params(1),
        name="conv_ln_pw2",
    )(u, u, x, w_dw, b_dw, ln_g, ln_b, w2, b2)


FFN_CHUNKS = (1,)

def _conv_ffn_kernel(x_ref, g_ref, wg_ref, wu_ref, wc_ref, wo_ref, o_ref,
                     h_scr, act_scr, g_scr, u_scr, tail_scr, *, tm, seq, nj, tn, chunks):
    i = pl.program_id(0)
    j = pl.program_id(1)
    tf = wg_ref.shape[1]
    first = (i * tm) % seq == 0
    rw = tm // chunks

    def project(slot, c):
        rows = slice(c * rw, (c + 1) * rw)
        h = h_scr[rows, :]
        g_scr[slot, SUBLANES + c * rw:SUBLANES + (c + 1) * rw, :] = jnp.dot(
            h, wg_ref[...], preferred_element_type=F32)
        u_scr[slot, rows, :] = jnp.dot(h, wu_ref[...], preferred_element_type=F32)

    def activate(jj, slot, c):
        if c == 0:
            g_scr[slot, 0:SUBLANES, :] = jnp.where(first, 0.0, tail_scr[jj])
        g = g_scr[slot, pl.ds(SUBLANES + c * rw, rw), :]
        g1 = g_scr[slot, pl.ds(SUBLANES - 1 + c * rw, rw), :]
        g2 = g_scr[slot, pl.ds(SUBLANES - 2 + c * rw, rw), :]
        if c == chunks - 1:
            tail_scr[jj] = g_scr[slot, pl.ds(tm, SUBLANES), :]
        gc = wc_ref[2:3, :] * g + wc_ref[1:2, :] * g1 + wc_ref[0:1, :] * g2
        act = (_silu(gc) * u_scr[slot, c * rw:(c + 1) * rw, :]).astype(BF16)
        act_scr[c * rw:(c + 1) * rw, pl.ds(pl.multiple_of(jj * tf, tf), tf)] = act

    @pl.when(jnp.logical_and(i == 0, j == 0))
    def _():
        tail_scr[...] = jnp.zeros(tail_scr.shape, F32)

    @pl.when(j == 0)
    def _():
        h_scr[...] = (_rms_normalize(x_ref[...]) * g_ref[...]).astype(BF16)
        for c in range(chunks):
            project(0, c)

    @pl.when(jnp.logical_and(j >= 1, j < nj))
    def _():
        slot = j % 2
        for c in range(chunks):
            project(slot, c)
            activate(j - 1, 1 - slot, c)

    @pl.when(j == nj)
    def _():
        for c in range(chunks):
            activate(nj - 1, (nj - 1) % 2, c)

    @pl.when(j >= nj)
    def _():
        col = pl.multiple_of((j - nj) * tn, tn)
        o_ref[...] = x_ref[:, pl.ds(col, tn)] + jnp.dot(
            act_scr[...], wo_ref[...], preferred_element_type=F32)


def _conv_ffn(x, gain, w_in, w_conv, w_out, *, tm, tf, tn, seq, chunks):
    m, d = x.shape
    f = w_out.shape[0]
    assert w_conv.shape[0] == 3 and f % tf == 0 and d % tn == 0 and tm % (chunks * SUBLANES) == 0
    nj = f // tf
    kern = functools.partial(_conv_ffn_kernel, tm=tm, seq=seq, nj=nj, tn=tn, chunks=chunks)
    proj = lambda j: jnp.minimum(j, nj - 1)
    actv = lambda j: jnp.clip(j - 1, 0, nj - 1)
    outc = lambda j: jnp.maximum(j - nj, 0)
    return pl.pallas_call(
        kern,
        grid=(m // tm, nj + d // tn),
        in_specs=[
            pl.BlockSpec((tm, d), lambda i, j: (i, 0)),
            pl.BlockSpec((1, d), lambda i, j: (0, 0)),
            pl.BlockSpec((d, tf), lambda i, j: (0, proj(j))),
            pl.BlockSpec((d, tf), lambda i, j: (0, nj + proj(j))),
            pl.BlockSpec((3, tf), lambda i, j: (0, actv(j))),
            pl.BlockSpec((f, tn), lambda i, j: (0, outc(j))),
        ],
        out_specs=pl.BlockSpec((tm, tn), lambda i, j: (i, outc(j))),
        out_shape=jax.ShapeDtypeStruct((m, d), F32),
        scratch_shapes=[
            pltpu.VMEM((tm, d), BF16),
            pltpu.VMEM((tm, f), BF16),
            pltpu.VMEM((2, SUBLANES + tm, tf), F32),
            pltpu.VMEM((2, tm, tf), F32),
            pltpu.VMEM((nj, SUBLANES, tf), F32),
        ],
        compiler_params=_params(2),
        name="conv_ffn",
    )(x, gain, w_in, w_in, w_conv, w_out)


def _qkv_kernel(x_ref, gkv_ref, gq_ref, wk_ref, wv_ref, wq_ref, kng_ref, qng_ref,
                qt_ref, ka_ref, vt_ref, km_ref, hkv_scr, hq_scr, *, tm, seq, hps, nbp):
    i = pl.program_id(0)

    @pl.when(pl.program_id(1) == 0)
    def _():
        xn = _rms_normalize(x_ref[...])
        hkv_scr[...] = (xn * gkv_ref[...]).astype(BF16)
        hq_scr[...] = (xn * gq_ref[...]).astype(BF16)

    nblk = tm // MOBA_BLOCK
    n_heads = x_ref.shape[1] // HEAD_DIM
    shape = (MOBA_BLOCK, LANES)
    lane = lax.broadcasted_iota(jnp.int32, shape, 1)
    ones_row = (lax.broadcasted_iota(jnp.int32, (VT_PAD, MOBA_BLOCK), 0) == 0).astype(BF16)

    for bb in range(nblk):
        rows = slice(bb * MOBA_BLOCK, (bb + 1) * MOBA_BLOCK)
        hkv = hkv_scr[rows, :]
        kk = jnp.dot(hkv, wk_ref[...], preferred_element_type=F32)
        vv = jnp.dot(hkv, wv_ref[...], preferred_element_type=F32)
        qq = jnp.dot(hq_scr[rows, :], wq_ref[...], preferred_element_type=F32)

        row0 = (i * tm) % seq + bb * MOBA_BLOCK
        onehot = (lane == row0 // MOBA_BLOCK).astype(BF16)
        pos = (row0 + lax.broadcasted_iota(jnp.int32, shape, 0)).astype(F32)

        for hh in range(hps):
            sl = slice(hh * HEAD_DIM, (hh + 1) * HEAD_DIM)
            kn = _rms_normalize(kk[:, sl]) * kng_ref[...]
            qt_ref[0, hh, :, rows] = (_rms_normalize(qq[:, sl]) * qng_ref[...]).T
            ka_ref[0, hh, rows, 0:HEAD_DIM] = kn.astype(BF16)
            hv = (jnp.zeros(shape, jnp.int32) + (pl.program_id(1) * hps + hh + 1)).astype(F32)
            a = jnp.exp2(hv * (-8.0 / n_heads)) * pos
            a_hi = a.astype(BF16)
            r1 = a - a_hi.astype(F32)
            a_mid = r1.astype(BF16)
            a_lo = (r1 - a_mid.astype(F32)).astype(BF16)
            ka_ref[0, hh, rows, HEAD_DIM:HEAD_DIM + LANES] = jnp.where(
                lane == nbp, a_hi, jnp.where(lane == nbp + 1, a_mid,
                                             jnp.where(lane == nbp + 2, a_lo, onehot)))
            vt_ref[0, hh, bb, 0:HEAD_DIM, :] = vv[:, sl].T.astype(BF16)
            vt_ref[0, hh, bb, HEAD_DIM:HEAD_DIM + VT_PAD, :] = ones_row
            km_ref[0, 0, hh, bb:bb + 1, :] = jnp.mean(kn, axis=0, keepdims=True)


def _qkv(x, g_kv, g_q, w_kv, w_q, k_norm_g, q_norm_g, *, tm, tn, batch, seq):
    m, d = x.shape
    assert tn % HEAD_DIM == 0 and tm % MOBA_BLOCK == 0 and seq % tm == 0
    nj = d // tn
    hps = tn // HEAD_DIM
    n_heads = d // HEAD_DIM
    tps = seq // tm
    nblk = tm // MOBA_BLOCK
    nbp = _padded_blocks(seq)
    assert nbp + ALIBI_TERMS <= LANES
    kern = functools.partial(_qkv_kernel, tm=tm, seq=seq, hps=hps, nbp=nbp)
    row_d = pl.BlockSpec((1, d), lambda i, j: (0, 0))
    row_h = pl.BlockSpec((1, HEAD_DIM), lambda i, j: (0, 0))
    return pl.pallas_call(
        kern,
        grid=(m // tm, nj),
        in_specs=[
            pl.BlockSpec((tm, d), lambda i, j: (i, 0)),
            row_d, row_d,
            pl.BlockSpec((d, tn), lambda i, j: (0, j)),
            pl.BlockSpec((d, tn), lambda i, j: (0, nj + j)),
            pl.BlockSpec((d, tn), lambda i, j: (0, j)),
            row_h, row_h,
        ],
        out_specs=[
            pl.BlockSpec((1, hps, HEAD_DIM, tm), lambda i, j: (i // tps, j, 0, i % tps)),
            pl.BlockSpec((1, hps, tm, HEAD_DIM + LANES), lambda i, j: (i // tps, j, i % tps, 0)),
            pl.BlockSpec((1, hps, nblk, HEAD_DIM + VT_PAD, MOBA_BLOCK),
                         lambda i, j: (i // tps, j, i % tps, 0, 0)),
            pl.BlockSpec((1, 1, hps, nblk, HEAD_DIM), lambda i, j: (i // tps, i % tps, j, 0, 0)),
        ],
        out_shape=[
            jax.ShapeDtypeStruct((batch, n_heads, HEAD_DIM, seq), F32),
            jax.ShapeDtypeStruct((batch, n_heads, seq, HEAD_DIM + LANES), BF16),
            jax.ShapeDtypeStruct((batch, n_heads, seq // MOBA_BLOCK, HEAD_DIM + VT_PAD, MOBA_BLOCK),
                                 BF16),
            jax.ShapeDtypeStruct((batch, tps, n_heads, nblk, HEAD_DIM), F32),
        ],
        scratch_shapes=[pltpu.VMEM((tm, d), BF16), pltpu.VMEM((tm, d), BF16)],
        compiler_params=_params(2),
        name="qkv",
    )(x, g_kv, g_q, w_kv, w_kv, w_q, k_norm_g, q_norm_g)


MOBA_HEADS = 8


def _moba_kernel(qt_ref, km_ref, ka_ref, vt_ref, o_ref, qat_scr, acc_scr, m_scr, *, n_heads, hb):
    hg = pl.program_id(1)
    i = pl.program_id(2)
    tq = MOBA_BLOCK
    nbp = km_ref.shape[2]
    heads = range(hb)

    blk = lax.broadcasted_iota(jnp.int32, (nbp, tq), 0)
    blk_f = blk.astype(F32)
    past = blk < i
    rest_rows = LANES - nbp
    ones_rows = (lax.broadcasted_iota(jnp.int32, (rest_rows, tq), 0) < ALIBI_TERMS).astype(BF16)
    for hh in heads:
        qt = qt_ref[0, hh]
        gate = jnp.dot(km_ref[0, hh], qt, precision=lax.Precision.HIGHEST,
                       preferred_element_type=F32)
        work = jnp.where(past, gate, NEG_INF)
        sel = jnp.zeros(gate.shape, jnp.bool_)
        for _ in range(TOP_K):
            mx = jnp.max(work, axis=0, keepdims=True)
            idx = jnp.min(jnp.where(work == mx, blk_f, float(LANES)), axis=0, keepdims=True)
            hit = blk_f == idx
            sel = jnp.logical_or(sel, hit)
            work = jnp.where(hit, -jnp.inf, work)
        allowed = jnp.logical_or(jnp.logical_and(sel, past), blk == i)
        qat_scr[hh, 0:HEAD_DIM, :] = (qt * HEAD_DIM ** -0.5).astype(BF16)
        qat_scr[hh, HEAD_DIM:HEAD_DIM + nbp, :] = jnp.where(allowed, 0.0, NEG_INF).astype(BF16)
        qat_scr[hh, HEAD_DIM + nbp:HEAD_DIM + LANES, :] = ones_rows

    t_q = (i * tq + lax.broadcasted_iota(jnp.int32, (1, tq), 1)).astype(F32)
    b = []
    for hh in heads:
        hv = (jnp.zeros((1, tq), jnp.int32) + (hg * hb + hh + 1)).astype(F32)
        b.append(jnp.exp2(hv * (-8.0 / n_heads)) * t_q)

    def scores(n, hh):
        start = pl.multiple_of(n * tq, tq)
        return jnp.dot(ka_ref[0, hh, pl.ds(start, tq), :], qat_scr[hh],
                       preferred_element_type=F32)

    def key_max(s):
        return jnp.max(s, axis=0, keepdims=True)

    odd = i % 2
    near = jnp.maximum(i - 1, 0)
    near_bias = jnp.where(odd == 1, 0.0, NEG_INF)
    causal = (lax.broadcasted_iota(jnp.int32, (tq, tq), 1)
              >= lax.broadcasted_iota(jnp.int32, (tq, tq), 0))
    s0 = [jnp.where(causal, scores(i, hh), NEG_INF) for hh in heads]
    s1 = [scores(near, hh) + near_bias for hh in heads]
    m0 = [jnp.maximum(key_max(s0[hh]), key_max(s1[hh])) - b[hh] for hh in heads]
    p0 = [jnp.exp(s0[hh] - (m0[hh] + b[hh])).astype(BF16) for hh in heads]
    p1 = [jnp.exp(s1[hh] - (m0[hh] + b[hh])).astype(BF16) for hh in heads]
    for hh in heads:
        acc_scr[hh] = (jnp.dot(vt_ref[0, hh, i], p0[hh], preferred_element_type=F32)
                       + jnp.dot(vt_ref[0, hh, near], p1[hh], preferred_element_type=F32))
        m_scr[hh] = m0[hh]

    rest = i - odd

    def pair(t, carry):
        na = rest - 1 - 2 * t
        nb = na - 1
        sa = [scores(na, hh) for hh in heads]
        sb = [scores(nb, hh) for hh in heads]
        m_prev = [m_scr[hh] for hh in heads]
        m_new = [jnp.maximum(m_prev[hh], jnp.maximum(key_max(sa[hh]), key_max(sb[hh])) - b[hh])
                 for hh in heads]
        alpha = [jnp.exp(m_prev[hh] - m_new[hh]) for hh in heads]
        pa = [jnp.exp(sa[hh] - (m_new[hh] + b[hh])).astype(BF16) for hh in heads]
        pb = [jnp.exp(sb[hh] - (m_new[hh] + b[hh])).astype(BF16) for hh in heads]
        for hh in heads:
            acc_scr[hh] = (alpha[hh] * acc_scr[hh]
                           + jnp.dot(vt_ref[0, hh, na], pa[hh], preferred_element_type=F32)
                           + jnp.dot(vt_ref[0, hh, nb], pb[hh], preferred_element_type=F32))
            m_scr[hh] = m_new[hh]
        return carry

    lax.fori_loop(0, rest // 2, pair, 0)
    for hh in heads:
        acc = acc_scr[hh]
        ot = acc[0:HEAD_DIM, :] / acc[HEAD_DIM:HEAD_DIM + 1, :]
        o_ref[:, hh * HEAD_DIM:(hh + 1) * HEAD_DIM] = ot.T.astype(o_ref.dtype)


def _moba(qt, km, ka, vt, *, seq):
    batch, n_heads, _, _ = qt.shape
    d = n_heads * HEAD_DIM
    nb = seq // MOBA_BLOCK
    nbp = km.shape[2]
    assert nbp == _padded_blocks(seq) and nbp + ALIBI_TERMS <= LANES
    hb = MOBA_HEADS if n_heads % MOBA_HEADS == 0 else 1
    tq = MOBA_BLOCK
    wide = HEAD_DIM + LANES
    kern = functools.partial(_moba_kernel, n_heads=n_heads, hb=hb)
    return pl.pallas_call(
        kern,
        grid=(batch, n_heads // hb, nb),
        in_specs=[
            pl.BlockSpec((1, hb, HEAD_DIM, tq), lambda b, g, i: (b, g, 0, i)),
            pl.BlockSpec((1, hb, nbp, HEAD_DIM), lambda b, g, i: (b, g, 0, 0)),
            pl.BlockSpec((1, hb, seq, wide), lambda b, g, i: (b, g, 0, 0),
                         pipeline_mode=pl.Buffered(1)),
            pl.BlockSpec((1, hb, nb, HEAD_DIM + VT_PAD, tq), lambda b, g, i: (b, g, 0, 0, 0),
                         pipeline_mode=pl.Buffered(1)),
        ],
        out_specs=pl.BlockSpec((tq, hb * HEAD_DIM), lambda b, g, i: (b * nb + i, g)),
        out_shape=jax.ShapeDtypeStruct((batch * seq, d), BF16),
        scratch_shapes=[
            pltpu.VMEM((hb, wide, tq), BF16),
            pltpu.VMEM((hb, HEAD_DIM + VT_PAD, tq), F32),
            pltpu.VMEM((hb, 1, tq), F32),
        ],
        compiler_params=_params(3),
        name="moba",
    )(qt, km, ka, vt)


def _wo_residual_kernel(o_ref, w_ref, x_ref, out_ref):
    out_ref[...] = x_ref[...] + jnp.dot(o_ref[...], w_ref[...], preferred_element_type=F32)


def _wo_residual(o, w, x, *, tm, tn):
    m, d = x.shape
    return pl.pallas_call(
        _wo_residual_kernel,
        grid=(m // tm, d // tn),
        in_specs=[
            pl.BlockSpec((tm, d), lambda i, j: (i, 0)),
            pl.BlockSpec((d, tn), lambda i, j: (0, j)),
            pl.BlockSpec((tm, tn), lambda i, j: (i, j)),
        ],
        out_specs=pl.BlockSpec((tm, tn), lambda i, j: (i, j)),
        out_shape=jax.ShapeDtypeStruct((m, d), F32),
        compiler_params=_params(2),
        name="wo_residual",
    )(o, w, x)


def _tiles(seq, d, f):
    def pick(n, cands):
        for c in cands:
            if n % c == 0:
                return c
        raise ValueError(f"no tile for {n}")
    return dict(
        tm=pick(seq, (512, 256)),
        tm_conv=pick(seq, (256,)),
        tn=pick(d, (512, 256, 128)),
        tf=pick(f, (512, 256, 128)),
    )


def kernel(x, conv_norm_g, conv_w_pw1, conv_b_pw1, conv_w_dw, conv_b_dw, conv_ln_g, conv_ln_b,
           conv_w_pw2, conv_b_pw2, kv_norm_g, w_kv, k_norm_g, attn_norm_g, w_q, q_norm_g, w_o,
           ffn_norm_g, ffn_w_in, ffn_w_conv, ffn_w_out):
    batch, seq, d = x.shape
    depth = ffn_w_in.shape[0]
    n_a = conv_w_pw1.shape[0]
    f = ffn_w_out.shape[1]
    t = _tiles(seq, d, f)
    n_heads = d // HEAD_DIM
    nb = seq // MOBA_BLOCK
    row = lambda v: v.reshape(1, -1)

    xf = x.reshape(batch * seq, d)
    kv = None
    for layer in range(depth):
        if layer < n_a:
            a = layer
            u = _pw1_glu(xf, row(conv_norm_g[a]), conv_w_pw1[a].astype(BF16), row(conv_b_pw1[a]),
                         tm=t["tm"], tn=t["tn"])
            xf = _conv_ln_pw2(u, xf, conv_w_dw[a], row(conv_b_dw[a]), row(conv_ln_g[a]),
                              row(conv_ln_b[a]), conv_w_pw2[a].astype(BF16), row(conv_b_pw2[a]),
                              tm=t["tm_conv"], seq=seq)
        else:
            b = layer - n_a
            qt, ka, vt, km = _qkv(xf, row(kv_norm_g), row(attn_norm_g[b]), w_kv.astype(BF16),
                                  w_q[b].astype(BF16), row(k_norm_g), row(q_norm_g[b]),
                                  tm=t["tm"], tn=t["tn"], batch=batch, seq=seq)
            if kv is None:
                km = km.transpose(0, 2, 1, 3, 4).reshape(batch, n_heads, nb, HEAD_DIM)
                km = jnp.pad(km, ((0, 0), (0, 0), (0, -nb % BF16_SUBLANES), (0, 0)))
                kv = (ka, vt, km)
            ka, vt, km = kv
            o = _moba(qt, km, ka, vt, seq=seq)
            xf = _wo_residual(o, w_o[b].astype(BF16), xf, tm=t["tm"], tn=t["tn"])
        xf = _conv_ffn(xf, row(ffn_norm_g[layer]), ffn_w_in[layer].astype(BF16), ffn_w_conv[layer],
                       ffn_w_out[layer].astype(BF16), tm=t["tm"], tf=t["tf"], tn=t["tn"], seq=seq,
                       chunks=FFN_CHUNKS[layer % len(FFN_CHUNKS)])
    return xf.reshape(batch, seq, d)
```

```python
import functools

import jax
import jax.numpy as jnp
from jax import lax
from jax.experimental import pallas as pl
from jax.experimental.pallas import tpu as pltpu

EPS = 1e-6
NEG_INF = -1e30
HEAD_DIM = 128
MOBA_BLOCK = 256
TOP_K = 3
LANES = 128
SUBLANES = 8
BF16_SUBLANES = 16
VT_PAD = BF16_SUBLANES
ALIBI_TERMS = 3
VMEM_LIMIT_BYTES = 56 * 1024 * 1024

F32 = jnp.float32
BF16 = jnp.bfloat16


def _padded_blocks(seq):
    nb = seq // MOBA_BLOCK
    return -(-nb // BF16_SUBLANES) * BF16_SUBLANES


def _params(n_axes):
    return pltpu.CompilerParams(dimension_semantics=("arbitrary",) * n_axes,
                                vmem_limit_bytes=VMEM_LIMIT_BYTES)


def _rms_normalize(x):
    return x * lax.rsqrt(jnp.mean(x * x, axis=-1, keepdims=True) + EPS)


def _silu(x):
    return x * jax.nn.sigmoid(x)


def _pw1_glu_kernel(x_ref, g_ref, wa_ref, wg_ref, ba_ref, bg_ref, u_ref, h_scr):
    @pl.when(pl.program_id(1) == 0)
    def _():
        h_scr[...] = (_rms_normalize(x_ref[...]) * g_ref[...]).astype(BF16)

    h = h_scr[...]
    a = jnp.dot(h, wa_ref[...], preferred_element_type=F32) + ba_ref[...]
    gate = jnp.dot(h, wg_ref[...], preferred_element_type=F32) + bg_ref[...]
    u_ref[...] = a * jax.nn.sigmoid(gate)


def _pw1_glu(x, gain, w1, b1, *, tm, tn):
    m, d = x.shape
    nj = d // tn
    return pl.pallas_call(
        _pw1_glu_kernel,
        grid=(m // tm, nj),
        in_specs=[
            pl.BlockSpec((tm, d), lambda i, j: (i, 0)),
            pl.BlockSpec((1, d), lambda i, j: (0, 0)),
            pl.BlockSpec((d, tn), lambda i, j: (0, j)),
            pl.BlockSpec((d, tn), lambda i, j: (0, nj + j)),
            pl.BlockSpec((1, tn), lambda i, j: (0, j)),
            pl.BlockSpec((1, tn), lambda i, j: (0, nj + j)),
        ],
        out_specs=pl.BlockSpec((tm, tn), lambda i, j: (i, j)),
        out_shape=jax.ShapeDtypeStruct((m, d), F32),
        scratch_shapes=[pltpu.VMEM((tm, d), BF16)],
        compiler_params=_params(2),
        name="pw1_glu",
    )(x, gain, w1, w1, b1, b1)


CONV_HALO = 32
CONV_ROWS = 64


def _conv_ln_pw2_kernel(u_ref, uh_ref, x_ref, wdw_ref, bdw_ref, lng_ref, lnb_ref, w2_ref, b2_ref,
                        o_ref, ub_scr, y_scr, *, tm, seq, width):
    i = pl.program_id(0)
    d = u_ref.shape[1]
    first = (i * tm) % seq == 0

    @pl.when(first)
    def _():
        ub_scr[0:CONV_HALO, :] = jnp.zeros((CONV_HALO, d), F32)

    @pl.when(jnp.logical_not(first))
    def _():
        ub_scr[0:CONV_HALO, :] = uh_ref[...]

    ub_scr[CONV_HALO:CONV_HALO + tm, :] = u_ref[...]

    off0 = CONV_HALO - (width - 1)
    max_off = off0 + width - 1
    n_a = max_off // SUBLANES + 1
    z_rows = CONV_ROWS + SUBLANES
    win_rows = z_rows + (n_a - 1) * SUBLANES

    def strip(c, carry):
        col = pl.multiple_of(c * LANES, LANES)
        for r in range(tm // CONV_ROWS):
            base = r * CONV_ROWS
            rows = min(win_rows, CONV_HALO + tm - base)
            win = ub_scr[pl.ds(base, rows), pl.ds(col, LANES)]
            y = None
            for b in range(SUBLANES):
                z = None
                for a in range(n_a):
                    k = SUBLANES * a + b - off0
                    if 0 <= k < width:
                        lo = SUBLANES * a
                        hi = min(lo + z_rows, rows)
                        term = wdw_ref[pl.ds(k, 1), pl.ds(col, LANES)] * win[lo:hi, :]
                        if hi - lo < z_rows:
                            term = jnp.concatenate(
                                [term, jnp.zeros((z_rows - (hi - lo), LANES), F32)], axis=0)
                        z = term if z is None else z + term
                part = z[b:b + CONV_ROWS, :]
                y = part if y is None else y + part
            y_scr[pl.ds(base, CONV_ROWS), pl.ds(col, LANES)] = y
        return carry

    lax.fori_loop(0, d // LANES, strip, 0)

    y = y_scr[...] + bdw_ref[...]
    mu = jnp.mean(y, axis=-1, keepdims=True)
    yc = y - mu
    var = jnp.mean(yc * yc, axis=-1, keepdims=True)
    yn = yc * lax.rsqrt(var + EPS) * lng_ref[...] + lnb_ref[...]
    act = _silu(yn).astype(BF16)
    o_ref[...] = x_ref[...] + (jnp.dot(act, w2_ref[...], preferred_element_type=F32) + b2_ref[...])


def _conv_ln_pw2(u, x, w_dw, b_dw, ln_g, ln_b, w2, b2, *, tm, seq):
    m, d = u.shape
    width = w_dw.shape[0]
    assert width - 1 <= CONV_HALO and tm % CONV_ROWS == 0 and tm % CONV_HALO == 0
    hb = tm // CONV_HALO
    kern = functools.partial(_conv_ln_pw2_kernel, tm=tm, seq=seq, width=width)
    row = pl.BlockSpec((1, d), lambda i: (0, 0))
    return pl.pallas_call(
        kern,
        grid=(m // tm,),
        in_specs=[
            pl.BlockSpec((tm, d), lambda i: (i, 0)),
            pl.BlockSpec((CONV_HALO, d), lambda i: (jnp.maximum(i * hb - 1, 0), 0)),
            pl.BlockSpec((tm, d), lambda i: (i, 0)),
            pl.BlockSpec((width, d), lambda i: (0, 0)),
            row, row, row,
            pl.BlockSpec((d, d), lambda i: (0, 0)),
            row,
        ],
        out_specs=pl.BlockSpec((tm, d), lambda i: (i, 0)),
        out_shape=jax.ShapeDtypeStruct((m, d), F32),
        scratch_shapes=[pltpu.VMEM((CONV_HALO + tm, d), F32), pltpu.VMEM((tm, d), F32)],
        compiler_params=_params(1),
        name="conv_ln_pw2",
    )(u, u, x, w_dw, b_dw, ln_g, ln_b, w2, b2)


FFN_CHUNKS = (2, 1)

def _conv_ffn_kernel(x_ref, g_ref, wg_ref, wu_ref, wc_ref, wo_ref, o_ref,
                     h_scr, act_scr, g_scr, u_scr, tail_scr, *, tm, seq, nj, tn, chunks):
    i = pl.program_id(0)
    j = pl.program_id(1)
    tf = wg_ref.shape[1]
    first = (i * tm) % seq == 0
    cw = tf // chunks

    def project(slot, c):
        cols = slice(c * cw, (c + 1) * cw)
        h = h_scr[...]
        g_scr[slot, SUBLANES:SUBLANES + tm, cols] = jnp.dot(
            h, wg_ref[:, cols], preferred_element_type=F32)
        u_scr[slot, :, cols] = jnp.dot(h, wu_ref[:, cols], preferred_element_type=F32)

    def activate(jj, slot, c):
        cols = slice(c * cw, (c + 1) * cw)
        g_scr[slot, 0:SUBLANES, cols] = jnp.where(first, 0.0, tail_scr[jj, :, cols])
        g = g_scr[slot, pl.ds(SUBLANES, tm), cols]
        g1 = g_scr[slot, pl.ds(SUBLANES - 1, tm), cols]
        g2 = g_scr[slot, pl.ds(SUBLANES - 2, tm), cols]
        tail_scr[jj, :, cols] = g_scr[slot, pl.ds(tm, SUBLANES), cols]
        gc = wc_ref[2:3, cols] * g + wc_ref[1:2, cols] * g1 + wc_ref[0:1, cols] * g2
        act = (_silu(gc) * u_scr[slot, :, cols]).astype(BF16)
        act_scr[:, pl.ds(pl.multiple_of(jj * tf + c * cw, cw), cw)] = act

    @pl.when(jnp.logical_and(i == 0, j == 0))
    def _():
        tail_scr[...] = jnp.zeros(tail_scr.shape, F32)

    @pl.when(j == 0)
    def _():
        h_scr[...] = (_rms_normalize(x_ref[...]) * g_ref[...]).astype(BF16)
        for c in range(chunks):
            project(0, c)

    @pl.when(jnp.logical_and(j >= 1, j < nj))
    def _():
        slot = j % 2
        for c in range(chunks):
            project(slot, c)
            activate(j - 1, 1 - slot, c)

    @pl.when(j == nj)
    def _():
        for c in range(chunks):
            activate(nj - 1, (nj - 1) % 2, c)

    @pl.when(j >= nj)
    def _():
        col = pl.multiple_of((j - nj) * tn, tn)
        o_ref[...] = x_ref[:, pl.ds(col, tn)] + jnp.dot(
            act_scr[...], wo_ref[...], preferred_element_type=F32)


def _conv_ffn(x, gain, w_in, w_conv, w_out, *, tm, tf, tn, seq, chunks):
    m, d = x.shape
    f = w_out.shape[0]
    assert w_conv.shape[0] == 3 and f % tf == 0 and d % tn == 0 and tf % (chunks * LANES) == 0
    nj = f // tf
    kern = functools.partial(_conv_ffn_kernel, tm=tm, seq=seq, nj=nj, tn=tn, chunks=chunks)
    proj = lambda j: jnp.minimum(j, nj - 1)
    actv = lambda j: jnp.clip(j - 1, 0, nj - 1)
    outc = lambda j: jnp.maximum(j - nj, 0)
    return pl.pallas_call(
        kern,
        grid=(m // tm, nj + d // tn),
        in_specs=[
            pl.BlockSpec((tm, d), lambda i, j: (i, 0)),
            pl.BlockSpec((1, d), lambda i, j: (0, 0)),
            pl.BlockSpec((d, tf), lambda i, j: (0, proj(j))),
            pl.BlockSpec((d, tf), lambda i, j: (0, nj + proj(j))),
            pl.BlockSpec((3, tf), lambda i, j: (0, actv(j))),
            pl.BlockSpec((f, tn), lambda i, j: (0, outc(j))),
        ],
        out_specs=pl.BlockSpec((tm, tn), lambda i, j: (i, outc(j))),
        out_shape=jax.ShapeDtypeStruct((m, d), F32),
        scratch_shapes=[
            pltpu.VMEM((tm, d), BF16),
            pltpu.VMEM((tm, f), BF16),
            pltpu.VMEM((2, SUBLANES + tm, tf), F32),
            pltpu.VMEM((2, tm, tf), F32),
            pltpu.VMEM((nj, SUBLANES, tf), F32),
        ],
        compiler_params=_params(2),
        name="conv_ffn",
    )(x, gain, w_in, w_in, w_conv, w_out)


def _qkv_kernel(x_ref, gkv_ref, gq_ref, wk_ref, wv_ref, wq_ref, kng_ref, qng_ref,
                qt_ref, ka_ref, vt_ref, km_ref, hkv_scr, hq_scr, *, tm, seq, hps, nbp):
    i = pl.program_id(0)

    @pl.when(pl.program_id(1) == 0)
    def _():
        xn = _rms_normalize(x_ref[...])
        hkv_scr[...] = (xn * gkv_ref[...]).astype(BF16)
        hq_scr[...] = (xn * gq_ref[...]).astype(BF16)

    nblk = tm // MOBA_BLOCK
    n_heads = x_ref.shape[1] // HEAD_DIM
    shape = (MOBA_BLOCK, LANES)
    lane = lax.broadcasted_iota(jnp.int32, shape, 1)
    ones_row = (lax.broadcasted_iota(jnp.int32, (VT_PAD, MOBA_BLOCK), 0) == 0).astype(BF16)

    for bb in range(nblk):
        rows = slice(bb * MOBA_BLOCK, (bb + 1) * MOBA_BLOCK)
        hkv = hkv_scr[rows, :]
        kk = jnp.dot(hkv, wk_ref[...], preferred_element_type=F32)
        vv = jnp.dot(hkv, wv_ref[...], preferred_element_type=F32)
        qq = jnp.dot(hq_scr[rows, :], wq_ref[...], preferred_element_type=F32)

        row0 = (i * tm) % seq + bb * MOBA_BLOCK
        onehot = (lane == row0 // MOBA_BLOCK).astype(BF16)
        pos = (row0 + lax.broadcasted_iota(jnp.int32, shape, 0)).astype(F32)

        for hh in range(hps):
            sl = slice(hh * HEAD_DIM, (hh + 1) * HEAD_DIM)
            kn = _rms_normalize(kk[:, sl]) * kng_ref[...]
            qt_ref[0, hh, :, rows] = (_rms_normalize(qq[:, sl]) * qng_ref[...]).T
            ka_ref[0, hh, rows, 0:HEAD_DIM] = kn.astype(BF16)
            hv = (jnp.zeros(shape, jnp.int32) + (pl.program_id(1) * hps + hh + 1)).astype(F32)
            a = jnp.exp2(hv * (-8.0 / n_heads)) * pos
            a_hi = a.astype(BF16)
            r1 = a - a_hi.astype(F32)
            a_mid = r1.astype(BF16)
            a_lo = (r1 - a_mid.astype(F32)).astype(BF16)
            ka_ref[0, hh, rows, HEAD_DIM:HEAD_DIM + LANES] = jnp.where(
                lane == nbp, a_hi, jnp.where(lane == nbp + 1, a_mid,
                                             jnp.where(lane == nbp + 2, a_lo, onehot)))
            vt_ref[0, hh, bb, 0:HEAD_DIM, :] = vv[:, sl].T.astype(BF16)
            vt_ref[0, hh, bb, HEAD_DIM:HEAD_DIM + VT_PAD, :] = ones_row
            km_ref[0, 0, hh, bb:bb + 1, :] = jnp.mean(kn, axis=0, keepdims=True)


def _qkv(x, g_kv, g_q, w_kv, w_q, k_norm_g, q_norm_g, *, tm, tn, batch, seq):
    m, d = x.shape
    assert tn % HEAD_DIM == 0 and tm % MOBA_BLOCK == 0 and seq % tm == 0
    nj = d // tn
    hps = tn // HEAD_DIM
    n_heads = d // HEAD_DIM
    tps = seq // tm
    nblk = tm // MOBA_BLOCK
    nbp = _padded_blocks(seq)
    assert nbp + ALIBI_TERMS <= LANES
    kern = functools.partial(_qkv_kernel, tm=tm, seq=seq, hps=hps, nbp=nbp)
    row_d = pl.BlockSpec((1, d), lambda i, j: (0, 0))
    row_h = pl.BlockSpec((1, HEAD_DIM), lambda i, j: (0, 0))
    return pl.pallas_call(
        kern,
        grid=(m // tm, nj),
        in_specs=[
            pl.BlockSpec((tm, d), lambda i, j: (i, 0)),
            row_d, row_d,
            pl.BlockSpec((d, tn), lambda i, j: (0, j)),
            pl.BlockSpec((d, tn), lambda i, j: (0, nj + j)),
            pl.BlockSpec((d, tn), lambda i, j: (0, j)),
            row_h, row_h,
        ],
        out_specs=[
            pl.BlockSpec((1, hps, HEAD_DIM, tm), lambda i, j: (i // tps, j, 0, i % tps)),
            pl.BlockSpec((1, hps, tm, HEAD_DIM + LANES), lambda i, j: (i // tps, j, i % tps, 0)),
            pl.BlockSpec((1, hps, nblk, HEAD_DIM + VT_PAD, MOBA_BLOCK),
                         lambda i, j: (i // tps, j, i % tps, 0, 0)),
            pl.BlockSpec((1, 1, hps, nblk, HEAD_DIM), lambda i, j: (i // tps, i % tps, j, 0, 0)),
        ],
        out_shape=[
            jax.ShapeDtypeStruct((batch, n_heads, HEAD_DIM, seq), F32),
            jax.ShapeDtypeStruct((batch, n_heads, seq, HEAD_DIM + LANES), BF16),
            jax.ShapeDtypeStruct((batch, n_heads, seq // MOBA_BLOCK, HEAD_DIM + VT_PAD, MOBA_BLOCK),
                                 BF16),
            jax.ShapeDtypeStruct((batch, tps, n_heads, nblk, HEAD_DIM), F32),
        ],
        scratch_shapes=[pltpu.VMEM((tm, d), BF16), pltpu.VMEM((tm, d), BF16)],
        compiler_params=_params(2),
        name="qkv",
    )(x, g_kv, g_q, w_kv, w_kv, w_q, k_norm_g, q_norm_g)


MOBA_HEADS = 8


def _moba_kernel(qt_ref, km_ref, ka_ref, vt_ref, o_ref, qat_scr, acc_scr, m_scr, *, n_heads, hb):
    hg = pl.program_id(1)
    i = pl.program_id(2)
    tq = MOBA_BLOCK
    nbp = km_ref.shape[2]
    heads = range(hb)

    blk = lax.broadcasted_iota(jnp.int32, (nbp, tq), 0)
    blk_f = blk.astype(F32)
    past = blk < i
    rest_rows = LANES - nbp
    ones_rows = (lax.broadcasted_iota(jnp.int32, (rest_rows, tq), 0) < ALIBI_TERMS).astype(BF16)
    for hh in heads:
        qt = qt_ref[0, hh]
        gate = jnp.dot(km_ref[0, hh], qt, precision=lax.Precision.HIGHEST,
                       preferred_element_type=F32)
        work = jnp.where(past, gate, NEG_INF)
        sel = jnp.zeros(gate.shape, jnp.bool_)
        for _ in range(TOP_K):
            mx = jnp.max(work, axis=0, keepdims=True)
            idx = jnp.min(jnp.where(work == mx, blk_f, float(LANES)), axis=0, keepdims=True)
            hit = blk_f == idx
            sel = jnp.logical_or(sel, hit)
            work = jnp.where(hit, -jnp.inf, work)
        allowed = jnp.logical_or(jnp.logical_and(sel, past), blk == i)
        qat_scr[hh, 0:HEAD_DIM, :] = (qt * HEAD_DIM ** -0.5).astype(BF16)
        qat_scr[hh, HEAD_DIM:HEAD_DIM + nbp, :] = jnp.where(allowed, 0.0, NEG_INF).astype(BF16)
        qat_scr[hh, HEAD_DIM + nbp:HEAD_DIM + LANES, :] = ones_rows

    t_q = (i * tq + lax.broadcasted_iota(jnp.int32, (1, tq), 1)).astype(F32)
    b = []
    for hh in heads:
        hv = (jnp.zeros((1, tq), jnp.int32) + (hg * hb + hh + 1)).astype(F32)
        b.append(jnp.exp2(hv * (-8.0 / n_heads)) * t_q)

    def scores(n, hh):
        start = pl.multiple_of(n * tq, tq)
        return jnp.dot(ka_ref[0, hh, pl.ds(start, tq), :], qat_scr[hh],
                       preferred_element_type=F32)

    def key_max(s):
        return jnp.max(s, axis=0, keepdims=True)

    odd = i % 2
    near = jnp.maximum(i - 1, 0)
    near_bias = jnp.where(odd == 1, 0.0, NEG_INF)
    causal = (lax.broadcasted_iota(jnp.int32, (tq, tq), 1)
              >= lax.broadcasted_iota(jnp.int32, (tq, tq), 0))
    s0 = [jnp.where(causal, scores(i, hh), NEG_INF) for hh in heads]
    s1 = [scores(near, hh) + near_bias for hh in heads]
    m0 = [jnp.maximum(key_max(s0[hh]), key_max(s1[hh])) - b[hh] for hh in heads]
    p0 = [jnp.exp(s0[hh] - (m0[hh] + b[hh])).astype(BF16) for hh in heads]
    p1 = [jnp.exp(s1[hh] - (m0[hh] + b[hh])).astype(BF16) for hh in heads]
    for hh in heads:
        acc_scr[hh] = (jnp.dot(vt_ref[0, hh, i], p0[hh], preferred_element_type=F32)
                       + jnp.dot(vt_ref[0, hh, near], p1[hh], preferred_element_type=F32))
        m_scr[hh] = m0[hh]

    rest = i - odd

    def pair(t, carry):
        na = rest - 1 - 2 * t
        nb = na - 1
        sa = [scores(na, hh) for hh in heads]
        sb = [scores(nb, hh) for hh in heads]
        m_prev = [m_scr[hh] for hh in heads]
        m_new = [jnp.maximum(m_prev[hh], jnp.maximum(key_max(sa[hh]), key_max(sb[hh])) - b[hh])
                 for hh in heads]
        alpha = [jnp.exp(m_prev[hh] - m_new[hh]) for hh in heads]
        pa = [jnp.exp(sa[hh] - (m_new[hh] + b[hh])).astype(BF16) for hh in heads]
        pb = [jnp.exp(sb[hh] - (m_new[hh] + b[hh])).astype(BF16) for hh in heads]
        for hh in heads:
            acc_scr[hh] = (alpha[hh] * acc_scr[hh]
                           + jnp.dot(vt_ref[0, hh, na], pa[hh], preferred_element_type=F32)
                           + jnp.dot(vt_ref[0, hh, nb], pb[hh], preferred_element_type=F32))
            m_scr[hh] = m_new[hh]
        return carry

    lax.fori_loop(0, rest // 2, pair, 0)
    for hh in heads:
        acc = acc_scr[hh]
        ot = acc[0:HEAD_DIM, :] / acc[HEAD_DIM:HEAD_DIM + 1, :]
        o_ref[:, hh * HEAD_DIM:(hh + 1) * HEAD_DIM] = ot.T.astype(o_ref.dtype)


def _moba(qt, km, ka, vt, *, seq):
    batch, n_heads, _, _ = qt.shape
    d = n_heads * HEAD_DIM
    nb = seq // MOBA_BLOCK
    nbp = km.shape[2]
    assert nbp == _padded_blocks(seq) and nbp + ALIBI_TERMS <= LANES
    hb = MOBA_HEADS if n_heads % MOBA_HEADS == 0 else 1
    tq = MOBA_BLOCK
    wide = HEAD_DIM + LANES
    kern = functools.partial(_moba_kernel, n_heads=n_heads, hb=hb)
    return pl.pallas_call(
        kern,
        grid=(batch, n_heads // hb, nb),
        in_specs=[
            pl.BlockSpec((1, hb, HEAD_DIM, tq), lambda b, g, i: (b, g, 0, i)),
            pl.BlockSpec((1, hb, nbp, HEAD_DIM), lambda b, g, i: (b, g, 0, 0)),
            pl.BlockSpec((1, hb, seq, wide), lambda b, g, i: (b, g, 0, 0),
                         pipeline_mode=pl.Buffered(1)),
            pl.BlockSpec((1, hb, nb, HEAD_DIM + VT_PAD, tq), lambda b, g, i: (b, g, 0, 0, 0),
                         pipeline_mode=pl.Buffered(1)),
        ],
        out_specs=pl.BlockSpec((tq, hb * HEAD_DIM), lambda b, g, i: (b * nb + i, g)),
        out_shape=jax.ShapeDtypeStruct((batch * seq, d), BF16),
        scratch_shapes=[
            pltpu.VMEM((hb, wide, tq), BF16),
            pltpu.VMEM((hb, HEAD_DIM + VT_PAD, tq), F32),
            pltpu.VMEM((hb, 1, tq), F32),
        ],
        compiler_params=_params(3),
        name="moba",
    )(qt, km, ka, vt)


def _wo_residual_kernel(o_ref, w_ref, x_ref, out_ref):
    out_ref[...] = x_ref[...] + jnp.dot(o_ref[...], w_ref[...], preferred_element_type=F32)


def _wo_residual(o, w, x, *, tm, tn):
    m, d = x.shape
    return pl.pallas_call(
        _wo_residual_kernel,
        grid=(m // tm, d // tn),
        in_specs=[
            pl.BlockSpec((tm, d), lambda i, j: (i, 0)),
            pl.BlockSpec((d, tn), lambda i, j: (0, j)),
            pl.BlockSpec((tm, tn), lambda i, j: (i, j)),
        ],
        out_specs=pl.BlockSpec((tm, tn), lambda i, j: (i, j)),
        out_shape=jax.ShapeDtypeStruct((m, d), F32),
        compiler_params=_params(2),
        name="wo_residual",
    )(o, w, x)


def _tiles(seq, d, f):
    def pick(n, cands):
        for c in cands:
            if n % c == 0:
                return c
        raise ValueError(f"no tile for {n}")
    return dict(
        tm=pick(seq, (512, 256)),
        tm_conv=pick(seq, (256,)),
        tn=pick(d, (512, 256, 128)),
        tf=pick(f, (512, 256, 128)),
    )


def kernel(x, conv_norm_g, conv_w_pw1, conv_b_pw1, conv_w_dw, conv_b_dw, conv_ln_g, conv_ln_b,
           conv_w_pw2, conv_b_pw2, kv_norm_g, w_kv, k_norm_g, attn_norm_g, w_q, q_norm_g, w_o,
           ffn_norm_g, ffn_w_in, ffn_w_conv, ffn_w_out):
    batch, seq, d = x.shape
    depth = ffn_w_in.shape[0]
    n_a = conv_w_pw1.shape[0]
    f = ffn_w_out.shape[1]
    t = _tiles(seq, d, f)
    n_heads = d // HEAD_DIM
    nb = seq // MOBA_BLOCK
    row = lambda v: v.reshape(1, -1)

    xf = x.reshape(batch * seq, d)
    kv = None
    for layer in range(depth):
        if layer < n_a:
            a = layer
            u = _pw1_glu(xf, row(conv_norm_g[a]), conv_w_pw1[a].astype(BF16), row(conv_b_pw1[a]),
                         tm=t["tm"], tn=t["tn"])
            xf = _conv_ln_pw2(u, xf, conv_w_dw[a], row(conv_b_dw[a]), row(conv_ln_g[a]),
                              row(conv_ln_b[a]), conv_w_pw2[a].astype(BF16), row(conv_b_pw2[a]),
                              tm=t["tm_conv"], seq=seq)
        else:
            b = layer - n_a
            qt, ka, vt, km = _qkv(xf, row(kv_norm_g), row(attn_norm_g[b]), w_kv.astype(BF16),
                                  w_q[b].astype(BF16), row(k_norm_g), row(q_norm_g[b]),
                                  tm=t["tm"], tn=t["tn"], batch=batch, seq=seq)
            if kv is None:
                km = km.transpose(0, 2, 1, 3, 4).reshape(batch, n_heads, nb, HEAD_DIM)
                km = jnp.pad(km, ((0, 0), (0, 0), (0, -nb % BF16_SUBLANES), (0, 0)))
                kv = (ka, vt, km)
            ka, vt, km = kv
            o = _moba(qt, km, ka, vt, seq=seq)
            xf = _wo_residual(o, w_o[b].astype(BF16), xf, tm=t["tm"], tn=d)
        xf = _conv_ffn(xf, row(ffn_norm_g[layer]), ffn_w_in[layer].astype(BF16), ffn_w_conv[layer],
                       ffn_w_out[layer].astype(BF16), tm=t["tm"], tf=t["tf"], tn=t["tn"], seq=seq,
                       chunks=FFN_CHUNKS[layer % len(FFN_CHUNKS)])
    return xf.reshape(batch, seq, d)
```

```python
import functools

import jax
import jax.numpy as jnp
from jax import lax
from jax.experimental import pallas as pl
from jax.experimental.pallas import tpu as pltpu

EPS = 1e-6
NEG_INF = -1e30
HEAD_DIM = 128
MOBA_BLOCK = 256
TOP_K = 3
LANES = 128
SUBLANES = 8
BF16_SUBLANES = 16
VT_PAD = BF16_SUBLANES
ALIBI_TERMS = 3
VMEM_LIMIT_BYTES = 56 * 1024 * 1024

F32 = jnp.float32
BF16 = jnp.bfloat16


def _padded_blocks(seq):
    nb = seq // MOBA_BLOCK
    return -(-nb // BF16_SUBLANES) * BF16_SUBLANES


def _params(n_axes):
    return pltpu.CompilerParams(dimension_semantics=("arbitrary",) * n_axes,
                                vmem_limit_bytes=VMEM_LIMIT_BYTES)


def _rms_normalize(x):
    return x * lax.rsqrt(jnp.mean(x * x, axis=-1, keepdims=True) + EPS)


def _silu(x):
    return x * jax.nn.sigmoid(x)


def _pw1_glu_kernel(x_ref, g_ref, wa_ref, wg_ref, ba_ref, bg_ref, u_ref, h_scr):
    @pl.when(pl.program_id(1) == 0)
    def _():
        h_scr[...] = (_rms_normalize(x_ref[...]) * g_ref[...]).astype(BF16)

    h = h_scr[...]
    a = jnp.dot(h, wa_ref[...], preferred_element_type=F32) + ba_ref[...]
    gate = jnp.dot(h, wg_ref[...], preferred_element_type=F32) + bg_ref[...]
    u_ref[...] = a * jax.nn.sigmoid(gate)


def _pw1_glu(x, gain, w1, b1, *, tm, tn):
    m, d = x.shape
    nj = d // tn
    return pl.pallas_call(
        _pw1_glu_kernel,
        grid=(m // tm, nj),
        in_specs=[
            pl.BlockSpec((tm, d), lambda i, j: (i, 0)),
            pl.BlockSpec((1, d), lambda i, j: (0, 0)),
            pl.BlockSpec((d, tn), lambda i, j: (0, j)),
            pl.BlockSpec((d, tn), lambda i, j: (0, nj + j)),
            pl.BlockSpec((1, tn), lambda i, j: (0, j)),
            pl.BlockSpec((1, tn), lambda i, j: (0, nj + j)),
        ],
        out_specs=pl.BlockSpec((tm, tn), lambda i, j: (i, j)),
        out_shape=jax.ShapeDtypeStruct((m, d), F32),
        scratch_shapes=[pltpu.VMEM((tm, d), BF16)],
        compiler_params=_params(2),
        name="pw1_glu",
    )(x, gain, w1, w1, b1, b1)


CONV_HALO = 32
CONV_ROWS = 64


def _conv_ln_pw2_kernel(u_ref, uh_ref, x_ref, wdw_ref, bdw_ref, lng_ref, lnb_ref, w2_ref, b2_ref,
                        o_ref, ub_scr, y_scr, *, tm, seq, width):
    i = pl.program_id(0)
    d = u_ref.shape[1]
    first = (i * tm) % seq == 0

    @pl.when(first)
    def _():
        ub_scr[0:CONV_HALO, :] = jnp.zeros((CONV_HALO, d), F32)

    @pl.when(jnp.logical_not(first))
    def _():
        ub_scr[0:CONV_HALO, :] = uh_ref[...]

    ub_scr[CONV_HALO:CONV_HALO + tm, :] = u_ref[...]

    off0 = CONV_HALO - (width - 1)
    max_off = off0 + width - 1
    n_a = max_off // SUBLANES + 1
    z_rows = CONV_ROWS + SUBLANES
    win_rows = z_rows + (n_a - 1) * SUBLANES

    def strip(c, carry):
        col = pl.multiple_of(c * LANES, LANES)
        for r in range(tm // CONV_ROWS):
            base = r * CONV_ROWS
            rows = min(win_rows, CONV_HALO + tm - base)
            win = ub_scr[pl.ds(base, rows), pl.ds(col, LANES)]
            y = None
            for b in range(SUBLANES):
                z = None
                for a in range(n_a):
                    k = SUBLANES * a + b - off0
                    if 0 <= k < width:
                        lo = SUBLANES * a
                        hi = min(lo + z_rows, rows)
                        term = wdw_ref[pl.ds(k, 1), pl.ds(col, LANES)] * win[lo:hi, :]
                        if hi - lo < z_rows:
                            term = jnp.concatenate(
                                [term, jnp.zeros((z_rows - (hi - lo), LANES), F32)], axis=0)
                        z = term if z is None else z + term
                part = z[b:b + CONV_ROWS, :]
                y = part if y is None else y + part
            y_scr[pl.ds(base, CONV_ROWS), pl.ds(col, LANES)] = y
        return carry

    lax.fori_loop(0, d // LANES, strip, 0)

    y = y_scr[...] + bdw_ref[...]
    mu = jnp.mean(y, axis=-1, keepdims=True)
    yc = y - mu
    var = jnp.mean(yc * yc, axis=-1, keepdims=True)
    yn = yc * lax.rsqrt(var + EPS) * lng_ref[...] + lnb_ref[...]
    act = _silu(yn).astype(BF16)
    o_ref[...] = x_ref[...] + (jnp.dot(act, w2_ref[...], preferred_element_type=F32) + b2_ref[...])


def _conv_ln_pw2(u, x, w_dw, b_dw, ln_g, ln_b, w2, b2, *, tm, seq):
    m, d = u.shape
    width = w_dw.shape[0]
    assert width - 1 <= CONV_HALO and tm % CONV_ROWS == 0 and tm % CONV_HALO == 0
    hb = tm // CONV_HALO
    kern = functools.partial(_conv_ln_pw2_kernel, tm=tm, seq=seq, width=width)
    row = pl.BlockSpec((1, d), lambda i: (0, 0))
    return pl.pallas_call(
        kern,
        grid=(m // tm,),
        in_specs=[
            pl.BlockSpec((tm, d), lambda i: (i, 0)),
            pl.BlockSpec((CONV_HALO, d), lambda i: (jnp.maximum(i * hb - 1, 0), 0)),
            pl.BlockSpec((tm, d), lambda i: (i, 0)),
            pl.BlockSpec((width, d), lambda i: (0, 0)),
            row, row, row,
            pl.BlockSpec((d, d), lambda i: (0, 0)),
            row,
        ],
        out_specs=pl.BlockSpec((tm, d), lambda i: (i, 0)),
        out_shape=jax.ShapeDtypeStruct((m, d), F32),
        scratch_shapes=[pltpu.VMEM((CONV_HALO + tm, d), F32), pltpu.VMEM((tm, d), F32)],
        compiler_params=_params(1),
        name="conv_ln_pw2",
    )(u, u, x, w_dw, b_dw, ln_g, ln_b, w2, b2)


FFN_CHUNKS = (1,)

def _conv_ffn_kernel(x_ref, g_ref, wg_ref, wu_ref, wc_ref, wo_ref, o_ref,
                     h_scr, act_scr, g_scr, u_scr, tail_scr, *, tm, seq, nj, tn, chunks):
    i = pl.program_id(0)
    j = pl.program_id(1)
    tf = wg_ref.shape[1]
    first = (i * tm) % seq == 0
    cw = tf // chunks

    def project(slot, c):
        cols = slice(c * cw, (c + 1) * cw)
        h = h_scr[...]
        g_scr[slot, SUBLANES:SUBLANES + tm, cols] = jnp.dot(
            h, wg_ref[:, cols], preferred_element_type=F32)
        u_scr[slot, :, cols] = jnp.dot(h, wu_ref[:, cols], preferred_element_type=F32)

    def activate(jj, slot, c):
        cols = slice(c * cw, (c + 1) * cw)
        g_scr[slot, 0:SUBLANES, cols] = jnp.where(first, 0.0, tail_scr[jj, :, cols])
        g = g_scr[slot, pl.ds(SUBLANES, tm), cols]
        g1 = g_scr[slot, pl.ds(SUBLANES - 1, tm), cols]
        g2 = g_scr[slot, pl.ds(SUBLANES - 2, tm), cols]
        tail_scr[jj, :, cols] = g_scr[slot, pl.ds(tm, SUBLANES), cols]
        gc = wc_ref[2:3, cols] * g + wc_ref[1:2, cols] * g1 + wc_ref[0:1, cols] * g2
        act = (_silu(gc) * u_scr[slot, :, cols]).astype(BF16)
        act_scr[:, pl.ds(pl.multiple_of(jj * tf + c * cw, cw), cw)] = act

    @pl.when(jnp.logical_and(i == 0, j == 0))
    def _():
        tail_scr[...] = jnp.zeros(tail_scr.shape, F32)

    @pl.when(j == 0)
    def _():
        h_scr[...] = (_rms_normalize(x_ref[...]) * g_ref[...]).astype(BF16)
        for c in range(chunks):
            project(0, c)

    @pl.when(jnp.logical_and(j >= 1, j < nj))
    def _():
        slot = j % 2
        for c in range(chunks):
            project(slot, c)
            activate(j - 1, 1 - slot, c)

    @pl.when(j == nj)
    def _():
        for c in range(chunks):
            activate(nj - 1, (nj - 1) % 2, c)

    @pl.when(j >= nj)
    def _():
        col = pl.multiple_of((j - nj) * tn, tn)
        o_ref[...] = x_ref[:, pl.ds(col, tn)] + jnp.dot(
            act_scr[...], wo_ref[...], preferred_element_type=F32)


def _conv_ffn(x, gain, w_in, w_conv, w_out, *, tm, tf, tn, seq, chunks):
    m, d = x.shape
    f = w_out.shape[0]
    assert w_conv.shape[0] == 3 and f % tf == 0 and d % tn == 0 and tf % (chunks * LANES) == 0
    nj = f // tf
    kern = functools.partial(_conv_ffn_kernel, tm=tm, seq=seq, nj=nj, tn=tn, chunks=chunks)
    proj = lambda j: jnp.minimum(j, nj - 1)
    actv = lambda j: jnp.clip(j - 1, 0, nj - 1)
    outc = lambda j: jnp.maximum(j - nj, 0)
    wout = lambda j: jnp.where(j < nj - 1, d // tn - 1, outc(j))
    return pl.pallas_call(
        kern,
        grid=(m // tm, nj + d // tn),
        in_specs=[
            pl.BlockSpec((tm, d), lambda i, j: (i, 0)),
            pl.BlockSpec((1, d), lambda i, j: (0, 0)),
            pl.BlockSpec((d, tf), lambda i, j: (0, proj(j))),
            pl.BlockSpec((d, tf), lambda i, j: (0, nj + proj(j))),
            pl.BlockSpec((3, tf), lambda i, j: (0, actv(j))),
            pl.BlockSpec((f, tn), lambda i, j: (0, wout(j))),
        ],
        out_specs=pl.BlockSpec((tm, tn), lambda i, j: (i, outc(j))),
        out_shape=jax.ShapeDtypeStruct((m, d), F32),
        scratch_shapes=[
            pltpu.VMEM((tm, d), BF16),
            pltpu.VMEM((tm, f), BF16),
            pltpu.VMEM((2, SUBLANES + tm, tf), F32),
            pltpu.VMEM((2, tm, tf), F32),
            pltpu.VMEM((nj, SUBLANES, tf), F32),
        ],
        compiler_params=_params(2),
        name="conv_ffn",
    )(x, gain, w_in, w_in, w_conv, w_out)


def _qkv_kernel(x_ref, gkv_ref, gq_ref, wk_ref, wv_ref, wq_ref, kng_ref, qng_ref,
                qt_ref, ka_ref, vt_ref, km_ref, hkv_scr, hq_scr, *, tm, seq, hps, nbp):
    i = pl.program_id(0)

    @pl.when(pl.program_id(1) == 0)
    def _():
        xn = _rms_normalize(x_ref[...])
        hkv_scr[...] = (xn * gkv_ref[...]).astype(BF16)
        hq_scr[...] = (xn * gq_ref[...]).astype(BF16)

    nblk = tm // MOBA_BLOCK
    n_heads = x_ref.shape[1] // HEAD_DIM
    shape = (MOBA_BLOCK, LANES)
    lane = lax.broadcasted_iota(jnp.int32, shape, 1)
    ones_row = (lax.broadcasted_iota(jnp.int32, (VT_PAD, MOBA_BLOCK), 0) == 0).astype(BF16)

    for bb in range(nblk):
        rows = slice(bb * MOBA_BLOCK, (bb + 1) * MOBA_BLOCK)
        hkv = hkv_scr[rows, :]
        kk = jnp.dot(hkv, wk_ref[...], preferred_element_type=F32)
        vv = jnp.dot(hkv, wv_ref[...], preferred_element_type=F32)
        qq = jnp.dot(hq_scr[rows, :], wq_ref[...], preferred_element_type=F32)

        row0 = (i * tm) % seq + bb * MOBA_BLOCK
        onehot = (lane == row0 // MOBA_BLOCK).astype(BF16)
        pos = (row0 + lax.broadcasted_iota(jnp.int32, shape, 0)).astype(F32)

        for hh in range(hps):
            sl = slice(hh * HEAD_DIM, (hh + 1) * HEAD_DIM)
            kn = _rms_normalize(kk[:, sl]) * kng_ref[...]
            qt_ref[0, hh, :, rows] = (_rms_normalize(qq[:, sl]) * qng_ref[...]).T
            ka_ref[0, hh, rows, 0:HEAD_DIM] = kn.astype(BF16)
            hv = (jnp.zeros(shape, jnp.int32) + (pl.program_id(1) * hps + hh + 1)).astype(F32)
            a = jnp.exp2(hv * (-8.0 / n_heads)) * pos
            a_hi = a.astype(BF16)
            r1 = a - a_hi.astype(F32)
            a_mid = r1.astype(BF16)
            a_lo = (r1 - a_mid.astype(F32)).astype(BF16)
            ka_ref[0, hh, rows, HEAD_DIM:HEAD_DIM + LANES] = jnp.where(
                lane == nbp, a_hi, jnp.where(lane == nbp + 1, a_mid,
                                             jnp.where(lane == nbp + 2, a_lo, onehot)))
            vt_ref[0, hh, bb, 0:HEAD_DIM, :] = vv[:, sl].T.astype(BF16)
            vt_ref[0, hh, bb, HEAD_DIM:HEAD_DIM + VT_PAD, :] = ones_row
            km_ref[0, 0, hh, bb:bb + 1, :] = jnp.mean(kn, axis=0, keepdims=True)


def _qkv(x, g_kv, g_q, w_kv, w_q, k_norm_g, q_norm_g, *, tm, tn, batch, seq):
    m, d = x.shape
    assert tn % HEAD_DIM == 0 and tm % MOBA_BLOCK == 0 and seq % tm == 0
    nj = d // tn
    hps = tn // HEAD_DIM
    n_heads = d // HEAD_DIM
    tps = seq // tm
    nblk = tm // MOBA_BLOCK
    nbp = _padded_blocks(seq)
    assert nbp + ALIBI_TERMS <= LANES
    kern = functools.partial(_qkv_kernel, tm=tm, seq=seq, hps=hps, nbp=nbp)
    row_d = pl.BlockSpec((1, d), lambda i, j: (0, 0))
    row_h = pl.BlockSpec((1, HEAD_DIM), lambda i, j: (0, 0))
    return pl.pallas_call(
        kern,
        grid=(m // tm, nj),
        in_specs=[
            pl.BlockSpec((tm, d), lambda i, j: (i, 0)),
            row_d, row_d,
            pl.BlockSpec((d, tn), lambda i, j: (0, j)),
            pl.BlockSpec((d, tn), lambda i, j: (0, nj + j)),
            pl.BlockSpec((d, tn), lambda i, j: (0, j)),
            row_h, row_h,
        ],
        out_specs=[
            pl.BlockSpec((1, hps, HEAD_DIM, tm), lambda i, j: (i // tps, j, 0, i % tps)),
            pl.BlockSpec((1, hps, tm, HEAD_DIM + LANES), lambda i, j: (i // tps, j, i % tps, 0)),
            pl.BlockSpec((1, hps, nblk, HEAD_DIM + VT_PAD, MOBA_BLOCK),
                         lambda i, j: (i // tps, j, i % tps, 0, 0)),
            pl.BlockSpec((1, 1, hps, nblk, HEAD_DIM), lambda i, j: (i // tps, i % tps, j, 0, 0)),
        ],
        out_shape=[
            jax.ShapeDtypeStruct((batch, n_heads, HEAD_DIM, seq), F32),
            jax.ShapeDtypeStruct((batch, n_heads, seq, HEAD_DIM + LANES), BF16),
            jax.ShapeDtypeStruct((batch, n_heads, seq // MOBA_BLOCK, HEAD_DIM + VT_PAD, MOBA_BLOCK),
                                 BF16),
            jax.ShapeDtypeStruct((batch, tps, n_heads, nblk, HEAD_DIM), F32),
        ],
        scratch_shapes=[pltpu.VMEM((tm, d), BF16), pltpu.VMEM((tm, d), BF16)],
        compiler_params=_params(2),
        name="qkv",
    )(x, g_kv, g_q, w_kv, w_kv, w_q, k_norm_g, q_norm_g)


MOBA_HEADS = 8


def _moba_kernel(qt_ref, km_ref, ka_ref, vt_ref, o_ref, qat_scr, acc_scr, m_scr, *, n_heads, hb):
    hg = pl.program_id(1)
    i = pl.program_id(2)
    tq = MOBA_BLOCK
    nbp = km_ref.shape[2]
    heads = range(hb)

    blk = lax.broadcasted_iota(jnp.int32, (nbp, tq), 0)
    blk_f = blk.astype(F32)
    past = blk < i
    rest_rows = LANES - nbp
    ones_rows = (lax.broadcasted_iota(jnp.int32, (rest_rows, tq), 0) < ALIBI_TERMS).astype(BF16)
    for hh in heads:
        qt = qt_ref[0, hh]
        gate = jnp.dot(km_ref[0, hh], qt, precision=lax.Precision.HIGHEST,
                       preferred_element_type=F32)
        work = jnp.where(past, gate, NEG_INF)
        sel = jnp.zeros(gate.shape, jnp.bool_)
        for _ in range(TOP_K):
            mx = jnp.max(work, axis=0, keepdims=True)
            idx = jnp.min(jnp.where(work == mx, blk_f, float(LANES)), axis=0, keepdims=True)
            hit = blk_f == idx
            sel = jnp.logical_or(sel, hit)
            work = jnp.where(hit, -jnp.inf, work)
        allowed = jnp.logical_or(jnp.logical_and(sel, past), blk == i)
        qat_scr[hh, 0:HEAD_DIM, :] = (qt * HEAD_DIM ** -0.5).astype(BF16)
        qat_scr[hh, HEAD_DIM:HEAD_DIM + nbp, :] = jnp.where(allowed, 0.0, NEG_INF).astype(BF16)
        qat_scr[hh, HEAD_DIM + nbp:HEAD_DIM + LANES, :] = ones_rows

    t_q = (i * tq + lax.broadcasted_iota(jnp.int32, (1, tq), 1)).astype(F32)
    b = []
    for hh in heads:
        hv = (jnp.zeros((1, tq), jnp.int32) + (hg * hb + hh + 1)).astype(F32)
        b.append(jnp.exp2(hv * (-8.0 / n_heads)) * t_q)

    def scores(n, hh):
        start = pl.multiple_of(n * tq, tq)
        return jnp.dot(ka_ref[0, hh, pl.ds(start, tq), :], qat_scr[hh],
                       preferred_element_type=F32)

    def key_max(s):
        return jnp.max(s, axis=0, keepdims=True)

    odd = i % 2
    near = jnp.maximum(i - 1, 0)
    near_bias = jnp.where(odd == 1, 0.0, NEG_INF)
    causal = (lax.broadcasted_iota(jnp.int32, (tq, tq), 1)
              >= lax.broadcasted_iota(jnp.int32, (tq, tq), 0))
    s0 = [jnp.where(causal, scores(i, hh), NEG_INF) for hh in heads]
    s1 = [scores(near, hh) + near_bias for hh in heads]
    m0 = [jnp.maximum(key_max(s0[hh]), key_max(s1[hh])) - b[hh] for hh in heads]
    p0 = [jnp.exp(s0[hh] - (m0[hh] + b[hh])).astype(BF16) for hh in heads]
    p1 = [jnp.exp(s1[hh] - (m0[hh] + b[hh])).astype(BF16) for hh in heads]
    for hh in heads:
        acc_scr[hh] = (jnp.dot(vt_ref[0, hh, i], p0[hh], preferred_element_type=F32)
                       + jnp.dot(vt_ref[0, hh, near], p1[hh], preferred_element_type=F32))
        m_scr[hh] = m0[hh]

    rest = i - odd

    def pair(t, carry):
        na = rest - 1 - 2 * t
        nb = na - 1
        sa = [scores(na, hh) for hh in heads]
        sb = [scores(nb, hh) for hh in heads]
        m_prev = [m_scr[hh] for hh in heads]
        m_new = [jnp.maximum(m_prev[hh], jnp.maximum(key_max(sa[hh]), key_max(sb[hh])) - b[hh])
                 for hh in heads]
        alpha = [jnp.exp(m_prev[hh] - m_new[hh]) for hh in heads]
        pa = [jnp.exp(sa[hh] - (m_new[hh] + b[hh])).astype(BF16) for hh in heads]
        pb = [jnp.exp(sb[hh] - (m_new[hh] + b[hh])).astype(BF16) for hh in heads]
        for hh in heads:
            acc_scr[hh] = (alpha[hh] * acc_scr[hh]
                           + jnp.dot(vt_ref[0, hh, na], pa[hh], preferred_element_type=F32)
                           + jnp.dot(vt_ref[0, hh, nb], pb[hh], preferred_element_type=F32))
            m_scr[hh] = m_new[hh]
        return carry

    lax.fori_loop(0, rest // 2, pair, 0)
    for hh in heads:
        acc = acc_scr[hh]
        ot = acc[0:HEAD_DIM, :] / acc[HEAD_DIM:HEAD_DIM + 1, :]
        o_ref[:, hh * HEAD_DIM:(hh + 1) * HEAD_DIM] = ot.T.astype(o_ref.dtype)


def _moba(qt, km, ka, vt, *, seq):
    batch, n_heads, _, _ = qt.shape
    d = n_heads * HEAD_DIM
    nb = seq // MOBA_BLOCK
    nbp = km.shape[2]
    assert nbp == _padded_blocks(seq) and nbp + ALIBI_TERMS <= LANES
    hb = MOBA_HEADS if n_heads % MOBA_HEADS == 0 else 1
    tq = MOBA_BLOCK
    wide = HEAD_DIM + LANES
    kern = functools.partial(_moba_kernel, n_heads=n_heads, hb=hb)
    return pl.pallas_call(
        kern,
        grid=(batch, n_heads // hb, nb),
        in_specs=[
            pl.BlockSpec((1, hb, HEAD_DIM, tq), lambda b, g, i: (b, g, 0, i)),
            pl.BlockSpec((1, hb, nbp, HEAD_DIM), lambda b, g, i: (b, g, 0, 0)),
            pl.BlockSpec((1, hb, seq, wide), lambda b, g, i: (b, g, 0, 0),
                         pipeline_mode=pl.Buffered(1)),
            pl.BlockSpec((1, hb, nb, HEAD_DIM + VT_PAD, tq), lambda b, g, i: (b, g, 0, 0, 0),
                         pipeline_mode=pl.Buffered(1)),
        ],
        out_specs=pl.BlockSpec((tq, hb * HEAD_DIM), lambda b, g, i: (b * nb + i, g)),
        out_shape=jax.ShapeDtypeStruct((batch * seq, d), BF16),
        scratch_shapes=[
            pltpu.VMEM((hb, wide, tq), BF16),
            pltpu.VMEM((hb, HEAD_DIM + VT_PAD, tq), F32),
            pltpu.VMEM((hb, 1, tq), F32),
        ],
        compiler_params=_params(3),
        name="moba",
    )(qt, km, ka, vt)


def _wo_residual_kernel(o_ref, w_ref, x_ref, out_ref):
    out_ref[...] = x_ref[...] + jnp.dot(o_ref[...], w_ref[...], preferred_element_type=F32)


def _wo_residual(o, w, x, *, tm, tn):
    m, d = x.shape
    return pl.pallas_call(
        _wo_residual_kernel,
        grid=(m // tm, d // tn),
        in_specs=[
            pl.BlockSpec((tm, d), lambda i, j: (i, 0)),
            pl.BlockSpec((d, tn), lambda i, j: (0, j)),
            pl.BlockSpec((tm, tn), lambda i, j: (i, j)),
        ],
        out_specs=pl.BlockSpec((tm, tn), lambda i, j: (i, j)),
        out_shape=jax.ShapeDtypeStruct((m, d), F32),
        compiler_params=_params(2),
        name="wo_residual",
    )(o, w, x)


def _tiles(seq, d, f):
    def pick(n, cands):
        for c in cands:
            if n % c == 0:
                return c
        raise ValueError(f"no tile for {n}")
    return dict(
        tm=pick(seq, (512, 256)),
        tm_conv=pick(seq, (256,)),
        tn=pick(d, (512, 256, 128)),
        tf=pick(f, (512, 256, 128)),
    )


def kernel(x, conv_norm_g, conv_w_pw1, conv_b_pw1, conv_w_dw, conv_b_dw, conv_ln_g, conv_ln_b,
           conv_w_pw2, conv_b_pw2, kv_norm_g, w_kv, k_norm_g, attn_norm_g, w_q, q_norm_g, w_o,
           ffn_norm_g, ffn_w_in, ffn_w_conv, ffn_w_out):
    batch, seq, d = x.shape
    depth = ffn_w_in.shape[0]
    n_a = conv_w_pw1.shape[0]
    f = ffn_w_out.shape[1]
    t = _tiles(seq, d, f)
    n_heads = d // HEAD_DIM
    nb = seq // MOBA_BLOCK
    row = lambda v: v.reshape(1, -1)

    xf = x.reshape(batch * seq, d)
    kv = None
    for layer in range(depth):
        if layer < n_a:
            a = layer
            u = _pw1_glu(xf, row(conv_norm_g[a]), conv_w_pw1[a].astype(BF16), row(conv_b_pw1[a]),
                         tm=t["tm"], tn=t["tn"])
            xf = _conv_ln_pw2(u, xf, conv_w_dw[a], row(conv_b_dw[a]), row(conv_ln_g[a]),
                              row(conv_ln_b[a]), conv_w_pw2[a].astype(BF16), row(conv_b_pw2[a]),
                              tm=t["tm_conv"], seq=seq)
        else:
            b = layer - n_a
            qt, ka, vt, km = _qkv(xf, row(kv_norm_g), row(attn_norm_g[b]), w_kv.astype(BF16),
                                  w_q[b].astype(BF16), row(k_norm_g), row(q_norm_g[b]),
                                  tm=t["tm"], tn=t["tn"], batch=batch, seq=seq)
            if kv is None:
                km = km.transpose(0, 2, 1, 3, 4).reshape(batch, n_heads, nb, HEAD_DIM)
                km = jnp.pad(km, ((0, 0), (0, 0), (0, -nb % BF16_SUBLANES), (0, 0)))
                kv = (ka, vt, km)
            ka, vt, km = kv
            o = _moba(qt, km, ka, vt, seq=seq)
            xf = _wo_residual(o, w_o[b].astype(BF16), xf, tm=t["tm"], tn=d)
        xf = _conv_ffn(xf, row(ffn_norm_g[layer]), ffn_w_in[layer].astype(BF16), ffn_w_conv[layer],
                       ffn_w_out[layer].astype(BF16), tm=t["tm"], tf=t["tf"], tn=t["tn"], seq=seq,
                       chunks=FFN_CHUNKS[layer % len(FFN_CHUNKS)])
    return xf.reshape(batch, seq, d)
```

```python
import functools

import jax
import jax.numpy as jnp
from jax import lax
from jax.experimental import pallas as pl
from jax.experimental.pallas import tpu as pltpu

EPS = 1e-6
NEG_INF = -1e30
HEAD_DIM = 128
MOBA_BLOCK = 256
TOP_K = 3
LANES = 128
SUBLANES = 8
BF16_SUBLANES = 16
VT_PAD = BF16_SUBLANES
ALIBI_TERMS = 3
VMEM_LIMIT_BYTES = 56 * 1024 * 1024

F32 = jnp.float32
BF16 = jnp.bfloat16


def _padded_blocks(seq):
    nb = seq // MOBA_BLOCK
    return -(-nb // BF16_SUBLANES) * BF16_SUBLANES


def _params(n_axes):
    return pltpu.CompilerParams(dimension_semantics=("arbitrary",) * n_axes,
                                vmem_limit_bytes=VMEM_LIMIT_BYTES)


def _rms_normalize(x):
    return x * lax.rsqrt(jnp.mean(x * x, axis=-1, keepdims=True) + EPS)


def _silu(x):
    return x * jax.nn.sigmoid(x)


def _pw1_glu_kernel(x_ref, g_ref, wa_ref, wg_ref, ba_ref, bg_ref, u_ref, h_scr):
    @pl.when(pl.program_id(1) == 0)
    def _():
        h_scr[...] = (_rms_normalize(x_ref[...]) * g_ref[...]).astype(BF16)

    h = h_scr[...]
    a = jnp.dot(h, wa_ref[...], preferred_element_type=F32) + ba_ref[...]
    gate = jnp.dot(h, wg_ref[...], preferred_element_type=F32) + bg_ref[...]
    u_ref[...] = a * jax.nn.sigmoid(gate)


def _pw1_glu(x, gain, w1, b1, *, tm, tn):
    m, d = x.shape
    nj = d // tn
    return pl.pallas_call(
        _pw1_glu_kernel,
        grid=(m // tm, nj),
        in_specs=[
            pl.BlockSpec((tm, d), lambda i, j: (i, 0)),
            pl.BlockSpec((1, d), lambda i, j: (0, 0)),
            pl.BlockSpec((d, tn), lambda i, j: (0, j)),
            pl.BlockSpec((d, tn), lambda i, j: (0, nj + j)),
            pl.BlockSpec((1, tn), lambda i, j: (0, j)),
            pl.BlockSpec((1, tn), lambda i, j: (0, nj + j)),
        ],
        out_specs=pl.BlockSpec((tm, tn), lambda i, j: (i, j)),
        out_shape=jax.ShapeDtypeStruct((m, d), F32),
        scratch_shapes=[pltpu.VMEM((tm, d), BF16)],
        compiler_params=_params(2),
        name="pw1_glu",
    )(x, gain, w1, w1, b1, b1)


CONV_HALO = 32
CONV_ROWS = 64


def _conv_ln_pw2_kernel(u_ref, uh_ref, x_ref, wdw_ref, bdw_ref, lng_ref, lnb_ref, w2_ref, b2_ref,
                        o_ref, ub_scr, y_scr, *, tm, seq, width):
    i = pl.program_id(0)
    d = u_ref.shape[1]
    first = (i * tm) % seq == 0

    @pl.when(first)
    def _():
        ub_scr[0:CONV_HALO, :] = jnp.zeros((CONV_HALO, d), F32)

    @pl.when(jnp.logical_not(first))
    def _():
        ub_scr[0:CONV_HALO, :] = uh_ref[...]

    ub_scr[CONV_HALO:CONV_HALO + tm, :] = u_ref[...]

    off0 = CONV_HALO - (width - 1)
    max_off = off0 + width - 1
    n_a = max_off // SUBLANES + 1
    z_rows = CONV_ROWS + SUBLANES
    win_rows = z_rows + (n_a - 1) * SUBLANES

    def strip(c, carry):
        col = pl.multiple_of(c * LANES, LANES)
        for r in range(tm // CONV_ROWS):
            base = r * CONV_ROWS
            rows = min(win_rows, CONV_HALO + tm - base)
            win = ub_scr[pl.ds(base, rows), pl.ds(col, LANES)]
            y = None
            for b in range(SUBLANES):
                z = None
                for a in range(n_a):
                    k = SUBLANES * a + b - off0
                    if 0 <= k < width:
                        lo = SUBLANES * a
                        hi = min(lo + z_rows, rows)
                        term = wdw_ref[pl.ds(k, 1), pl.ds(col, LANES)] * win[lo:hi, :]
                        if hi - lo < z_rows:
                            term = jnp.concatenate(
                                [term, jnp.zeros((z_rows - (hi - lo), LANES), F32)], axis=0)
                        z = term if z is None else z + term
                part = z[b:b + CONV_ROWS, :]
                y = part if y is None else y + part
            y_scr[pl.ds(base, CONV_ROWS), pl.ds(col, LANES)] = y
        return carry

    lax.fori_loop(0, d // LANES, strip, 0)

    y = y_scr[...] + bdw_ref[...]
    mu = jnp.mean(y, axis=-1, keepdims=True)
    yc = y - mu
    var = jnp.mean(yc * yc, axis=-1, keepdims=True)
    yn = yc * lax.rsqrt(var + EPS) * lng_ref[...] + lnb_ref[...]
    act = _silu(yn).astype(BF16)
    o_ref[...] = x_ref[...] + (jnp.dot(act, w2_ref[...], preferred_element_type=F32) + b2_ref[...])


def _conv_ln_pw2(u, x, w_dw, b_dw, ln_g, ln_b, w2, b2, *, tm, seq):
    m, d = u.shape
    width = w_dw.shape[0]
    assert width - 1 <= CONV_HALO and tm % CONV_ROWS == 0 and tm % CONV_HALO == 0
    hb = tm // CONV_HALO
    kern = functools.partial(_conv_ln_pw2_kernel, tm=tm, seq=seq, width=width)
    row = pl.BlockSpec((1, d), lambda i: (0, 0))
    return pl.pallas_call(
        kern,
        grid=(m // tm,),
        in_specs=[
            pl.BlockSpec((tm, d), lambda i: (i, 0)),
            pl.BlockSpec((CONV_HALO, d), lambda i: (jnp.maximum(i * hb - 1, 0), 0)),
            pl.BlockSpec((tm, d), lambda i: (i, 0)),
            pl.BlockSpec((width, d), lambda i: (0, 0)),
            row, row, row,
            pl.BlockSpec((d, d), lambda i: (0, 0)),
            row,
        ],
        out_specs=pl.BlockSpec((tm, d), lambda i: (i, 0)),
        out_shape=jax.ShapeDtypeStruct((m, d), F32),
        scratch_shapes=[pltpu.VMEM((CONV_HALO + tm, d), F32), pltpu.VMEM((tm, d), F32)],
        compiler_params=_params(1),
        name="conv_ln_pw2",
    )(u, u, x, w_dw, b_dw, ln_g, ln_b, w2, b2)


FFN_CHUNKS = (1,)

def _conv_ffn_kernel(x_ref, g_ref, wg_ref, wu_ref, wc_ref, wo_ref, o_ref,
                     h_scr, act_scr, g_scr, u_scr, tail_scr, *, tm, seq, nj, tn, chunks):
    i = pl.program_id(0)
    j = pl.program_id(1)
    tf = wg_ref.shape[1]
    first = (i * tm) % seq == 0
    cw = tf // chunks

    def project(slot, c):
        cols = slice(c * cw, (c + 1) * cw)
        h = h_scr[...]
        g_scr[slot, SUBLANES:SUBLANES + tm, cols] = jnp.dot(
            h, wg_ref[:, cols], preferred_element_type=F32)
        u_scr[slot, :, cols] = jnp.dot(h, wu_ref[:, cols], preferred_element_type=F32)

    def activate(jj, slot, c):
        cols = slice(c * cw, (c + 1) * cw)
        g_scr[slot, 0:SUBLANES, cols] = jnp.where(first, 0.0, tail_scr[jj, :, cols])
        g = g_scr[slot, pl.ds(SUBLANES, tm), cols]
        g1 = g_scr[slot, pl.ds(SUBLANES - 1, tm), cols]
        g2 = g_scr[slot, pl.ds(SUBLANES - 2, tm), cols]
        tail_scr[jj, :, cols] = g_scr[slot, pl.ds(tm, SUBLANES), cols]
        gc = wc_ref[2:3, cols] * g + wc_ref[1:2, cols] * g1 + wc_ref[0:1, cols] * g2
        act = (_silu(gc) * u_scr[slot, :, cols]).astype(BF16)
        act_scr[:, pl.ds(pl.multiple_of(jj * tf + c * cw, cw), cw)] = act

    @pl.when(jnp.logical_and(i == 0, j == 0))
    def _():
        tail_scr[...] = jnp.zeros(tail_scr.shape, F32)

    @pl.when(j == 0)
    def _():
        h_scr[...] = (_rms_normalize(x_ref[...]) * g_ref[...]).astype(BF16)
        for c in range(chunks):
            project(0, c)

    @pl.when(jnp.logical_and(j >= 1, j < nj))
    def _():
        slot = j % 2
        for c in range(chunks):
            project(slot, c)
            activate(j - 1, 1 - slot, c)

    @pl.when(j == nj)
    def _():
        for c in range(chunks):
            activate(nj - 1, (nj - 1) % 2, c)

    @pl.when(j >= nj)
    def _():
        col = pl.multiple_of((j - nj) * tn, tn)
        o_ref[...] = x_ref[:, pl.ds(col, tn)] + jnp.dot(
            act_scr[...], wo_ref[...], preferred_element_type=F32)


def _conv_ffn(x, gain, w_in, w_conv, w_out, *, tm, tf, tn, seq, chunks):
    m, d = x.shape
    f = w_out.shape[0]
    assert w_conv.shape[0] == 3 and f % tf == 0 and d % tn == 0 and tf % (chunks * LANES) == 0
    nj = f // tf
    kern = functools.partial(_conv_ffn_kernel, tm=tm, seq=seq, nj=nj, tn=tn, chunks=chunks)
    proj = lambda j: jnp.where(j < nj, j, 0)
    actv = lambda j: jnp.clip(j - 1, 0, nj - 1)
    outc = lambda j: jnp.maximum(j - nj, 0)
    wout = lambda j: jnp.where(j < nj - 1, d // tn - 1, outc(j))
    return pl.pallas_call(
        kern,
        grid=(m // tm, nj + d // tn),
        in_specs=[
            pl.BlockSpec((tm, d), lambda i, j: (i, 0)),
            pl.BlockSpec((1, d), lambda i, j: (0, 0)),
            pl.BlockSpec((d, tf), lambda i, j: (0, proj(j))),
            pl.BlockSpec((d, tf), lambda i, j: (0, nj + proj(j))),
            pl.BlockSpec((3, tf), lambda i, j: (0, actv(j))),
            pl.BlockSpec((f, tn), lambda i, j: (0, wout(j))),
        ],
        out_specs=pl.BlockSpec((tm, tn), lambda i, j: (i, outc(j))),
        out_shape=jax.ShapeDtypeStruct((m, d), F32),
        scratch_shapes=[
            pltpu.VMEM((tm, d), BF16),
            pltpu.VMEM((tm, f), BF16),
            pltpu.VMEM((2, SUBLANES + tm, tf), F32),
            pltpu.VMEM((2, tm, tf), F32),
            pltpu.VMEM((nj, SUBLANES, tf), F32),
        ],
        compiler_params=_params(2),
        name="conv_ffn",
    )(x, gain, w_in, w_in, w_conv, w_out)


def _qkv_kernel(x_ref, gkv_ref, gq_ref, wk_ref, wv_ref, wq_ref, kng_ref, qng_ref,
                qt_ref, ka_ref, vt_ref, km_ref, hkv_scr, hq_scr, *, tm, seq, hps, nbp):
    i = pl.program_id(0)

    @pl.when(pl.program_id(1) == 0)
    def _():
        xn = _rms_normalize(x_ref[...])
        hkv_scr[...] = (xn * gkv_ref[...]).astype(BF16)
        hq_scr[...] = (xn * gq_ref[...]).astype(BF16)

    nblk = tm // MOBA_BLOCK
    n_heads = x_ref.shape[1] // HEAD_DIM
    shape = (MOBA_BLOCK, LANES)
    lane = lax.broadcasted_iota(jnp.int32, shape, 1)
    ones_row = (lax.broadcasted_iota(jnp.int32, (VT_PAD, MOBA_BLOCK), 0) == 0).astype(BF16)

    for bb in range(nblk):
        rows = slice(bb * MOBA_BLOCK, (bb + 1) * MOBA_BLOCK)
        hkv = hkv_scr[rows, :]
        kk = jnp.dot(hkv, wk_ref[...], preferred_element_type=F32)
        vv = jnp.dot(hkv, wv_ref[...], preferred_element_type=F32)
        qq = jnp.dot(hq_scr[rows, :], wq_ref[...], preferred_element_type=F32)

        row0 = (i * tm) % seq + bb * MOBA_BLOCK
        onehot = (lane == row0 // MOBA_BLOCK).astype(BF16)
        pos = (row0 + lax.broadcasted_iota(jnp.int32, shape, 0)).astype(F32)

        for hh in range(hps):
            sl = slice(hh * HEAD_DIM, (hh + 1) * HEAD_DIM)
            kn = _rms_normalize(kk[:, sl]) * kng_ref[...]
            qt_ref[0, hh, :, rows] = (_rms_normalize(qq[:, sl]) * qng_ref[...]).T
            ka_ref[0, hh, rows, 0:HEAD_DIM] = kn.astype(BF16)
            hv = (jnp.zeros(shape, jnp.int32) + (pl.program_id(1) * hps + hh + 1)).astype(F32)
            a = jnp.exp2(hv * (-8.0 / n_heads)) * pos
            a_hi = a.astype(BF16)
            r1 = a - a_hi.astype(F32)
            a_mid = r1.astype(BF16)
            a_lo = (r1 - a_mid.astype(F32)).astype(BF16)
            ka_ref[0, hh, rows, HEAD_DIM:HEAD_DIM + LANES] = jnp.where(
                lane == nbp, a_hi, jnp.where(lane == nbp + 1, a_mid,
                                             jnp.where(lane == nbp + 2, a_lo, onehot)))
            vt_ref[0, hh, bb, 0:HEAD_DIM, :] = vv[:, sl].T.astype(BF16)
            vt_ref[0, hh, bb, HEAD_DIM:HEAD_DIM + VT_PAD, :] = ones_row
            km_ref[0, 0, hh, bb:bb + 1, :] = jnp.mean(kn, axis=0, keepdims=True)


def _qkv(x, g_kv, g_q, w_kv, w_q, k_norm_g, q_norm_g, *, tm, tn, batch, seq):
    m, d = x.shape
    assert tn % HEAD_DIM == 0 and tm % MOBA_BLOCK == 0 and seq % tm == 0
    nj = d // tn
    hps = tn // HEAD_DIM
    n_heads = d // HEAD_DIM
    tps = seq // tm
    nblk = tm // MOBA_BLOCK
    nbp = _padded_blocks(seq)
    assert nbp + ALIBI_TERMS <= LANES
    kern = functools.partial(_qkv_kernel, tm=tm, seq=seq, hps=hps, nbp=nbp)
    row_d = pl.BlockSpec((1, d), lambda i, j: (0, 0))
    row_h = pl.BlockSpec((1, HEAD_DIM), lambda i, j: (0, 0))
    return pl.pallas_call(
        kern,
        grid=(m // tm, nj),
        in_specs=[
            pl.BlockSpec((tm, d), lambda i, j: (i, 0)),
            row_d, row_d,
            pl.BlockSpec((d, tn), lambda i, j: (0, j)),
            pl.BlockSpec((d, tn), lambda i, j: (0, nj + j)),
            pl.BlockSpec((d, tn), lambda i, j: (0, j)),
            row_h, row_h,
        ],
        out_specs=[
            pl.BlockSpec((1, hps, HEAD_DIM, tm), lambda i, j: (i // tps, j, 0, i % tps)),
            pl.BlockSpec((1, hps, tm, HEAD_DIM + LANES), lambda i, j: (i // tps, j, i % tps, 0)),
            pl.BlockSpec((1, hps, nblk, HEAD_DIM + VT_PAD, MOBA_BLOCK),
                         lambda i, j: (i // tps, j, i % tps, 0, 0)),
            pl.BlockSpec((1, 1, hps, nblk, HEAD_DIM), lambda i, j: (i // tps, i % tps, j, 0, 0)),
        ],
        out_shape=[
            jax.ShapeDtypeStruct((batch, n_heads, HEAD_DIM, seq), F32),
            jax.ShapeDtypeStruct((batch, n_heads, seq, HEAD_DIM + LANES), BF16),
            jax.ShapeDtypeStruct((batch, n_heads, seq // MOBA_BLOCK, HEAD_DIM + VT_PAD, MOBA_BLOCK),
                                 BF16),
            jax.ShapeDtypeStruct((batch, tps, n_heads, nblk, HEAD_DIM), F32),
        ],
        scratch_shapes=[pltpu.VMEM((tm, d), BF16), pltpu.VMEM((tm, d), BF16)],
        compiler_params=_params(2),
        name="qkv",
    )(x, g_kv, g_q, w_kv, w_kv, w_q, k_norm_g, q_norm_g)


MOBA_HEADS = 8


def _moba_kernel(qt_ref, km_ref, ka_ref, vt_ref, o_ref, qat_scr, acc_scr, m_scr, *, n_heads, hb):
    hg = pl.program_id(1)
    i = pl.program_id(2)
    tq = MOBA_BLOCK
    nbp = km_ref.shape[2]
    heads = range(hb)

    blk = lax.broadcasted_iota(jnp.int32, (nbp, tq), 0)
    blk_f = blk.astype(F32)
    past = blk < i
    rest_rows = LANES - nbp
    ones_rows = (lax.broadcasted_iota(jnp.int32, (rest_rows, tq), 0) < ALIBI_TERMS).astype(BF16)
    for hh in heads:
        qt = qt_ref[0, hh]
        gate = jnp.dot(km_ref[0, hh], qt, precision=lax.Precision.HIGHEST,
                       preferred_element_type=F32)
        work = jnp.where(past, gate, NEG_INF)
        sel = jnp.zeros(gate.shape, jnp.bool_)
        for _ in range(TOP_K):
            mx = jnp.max(work, axis=0, keepdims=True)
            idx = jnp.min(jnp.where(work == mx, blk_f, float(LANES)), axis=0, keepdims=True)
            hit = blk_f == idx
            sel = jnp.logical_or(sel, hit)
            work = jnp.where(hit, -jnp.inf, work)
        allowed = jnp.logical_or(jnp.logical_and(sel, past), blk == i)
        qat_scr[hh, 0:HEAD_DIM, :] = (qt * HEAD_DIM ** -0.5).astype(BF16)
        qat_scr[hh, HEAD_DIM:HEAD_DIM + nbp, :] = jnp.where(allowed, 0.0, NEG_INF).astype(BF16)
        qat_scr[hh, HEAD_DIM + nbp:HEAD_DIM + LANES, :] = ones_rows

    t_q = (i * tq + lax.broadcasted_iota(jnp.int32, (1, tq), 1)).astype(F32)
    b = []
    for hh in heads:
        hv = (jnp.zeros((1, tq), jnp.int32) + (hg * hb + hh + 1)).astype(F32)
        b.append(jnp.exp2(hv * (-8.0 / n_heads)) * t_q)

    def scores(n, hh):
        start = pl.multiple_of(n * tq, tq)
        return jnp.dot(ka_ref[0, hh, pl.ds(start, tq), :], qat_scr[hh],
                       preferred_element_type=F32)

    def key_max(s):
        return jnp.max(s, axis=0, keepdims=True)

    odd = i % 2
    near = jnp.maximum(i - 1, 0)
    near_bias = jnp.where(odd == 1, 0.0, NEG_INF)
    causal = (lax.broadcasted_iota(jnp.int32, (tq, tq), 1)
              >= lax.broadcasted_iota(jnp.int32, (tq, tq), 0))
    s0 = [jnp.where(causal, scores(i, hh), NEG_INF) for hh in heads]
    s1 = [scores(near, hh) + near_bias for hh in heads]
    m0 = [jnp.maximum(key_max(s0[hh]), key_max(s1[hh])) - b[hh] for hh in heads]
    p0 = [jnp.exp(s0[hh] - (m0[hh] + b[hh])).astype(BF16) for hh in heads]
    p1 = [jnp.exp(s1[hh] - (m0[hh] + b[hh])).astype(BF16) for hh in heads]
    for hh in heads:
        acc_scr[hh] = (jnp.dot(vt_ref[0, hh, i], p0[hh], preferred_element_type=F32)
                       + jnp.dot(vt_ref[0, hh, near], p1[hh], preferred_element_type=F32))
        m_scr[hh] = m0[hh]

    rest = i - odd

    def pair(t, carry):
        na = rest - 1 - 2 * t
        nb = na - 1
        sa = [scores(na, hh) for hh in heads]
        sb = [scores(nb, hh) for hh in heads]
        m_prev = [m_scr[hh] for hh in heads]
        m_new = [jnp.maximum(m_prev[hh], jnp.maximum(key_max(sa[hh]), key_max(sb[hh])) - b[hh])
                 for hh in heads]
        alpha = [jnp.exp(m_prev[hh] - m_new[hh]) for hh in heads]
        pa = [jnp.exp(sa[hh] - (m_new[hh] + b[hh])).astype(BF16) for hh in heads]
        pb = [jnp.exp(sb[hh] - (m_new[hh] + b[hh])).astype(BF16) for hh in heads]
        for hh in heads:
            acc_scr[hh] = (alpha[hh] * acc_scr[hh]
                           + jnp.dot(vt_ref[0, hh, na], pa[hh], preferred_element_type=F32)
                           + jnp.dot(vt_ref[0, hh, nb], pb[hh], preferred_element_type=F32))
            m_scr[hh] = m_new[hh]
        return carry

    lax.fori_loop(0, rest // 2, pair, 0)
    for hh in heads:
        acc = acc_scr[hh]
        ot = acc[0:HEAD_DIM, :] / acc[HEAD_DIM:HEAD_DIM + 1, :]
        o_ref[:, hh * HEAD_DIM:(hh + 1) * HEAD_DIM] = ot.T.astype(o_ref.dtype)


def _moba(qt, km, ka, vt, *, seq):
    batch, n_heads, _, _ = qt.shape
    d = n_heads * HEAD_DIM
    nb = seq // MOBA_BLOCK
    nbp = km.shape[2]
    assert nbp == _padded_blocks(seq) and nbp + ALIBI_TERMS <= LANES
    hb = MOBA_HEADS if n_heads % MOBA_HEADS == 0 else 1
    tq = MOBA_BLOCK
    wide = HEAD_DIM + LANES
    kern = functools.partial(_moba_kernel, n_heads=n_heads, hb=hb)
    return pl.pallas_call(
        kern,
        grid=(batch, n_heads // hb, nb),
        in_specs=[
            pl.BlockSpec((1, hb, HEAD_DIM, tq), lambda b, g, i: (b, g, 0, i)),
            pl.BlockSpec((1, hb, nbp, HEAD_DIM), lambda b, g, i: (b, g, 0, 0)),
            pl.BlockSpec((1, hb, seq, wide), lambda b, g, i: (b, g, 0, 0)),
            pl.BlockSpec((1, hb, nb, HEAD_DIM + VT_PAD, tq), lambda b, g, i: (b, g, 0, 0, 0),
                         pipeline_mode=pl.Buffered(1)),
        ],
        out_specs=pl.BlockSpec((tq, hb * HEAD_DIM), lambda b, g, i: (b * nb + i, g)),
        out_shape=jax.ShapeDtypeStruct((batch * seq, d), BF16),
        scratch_shapes=[
            pltpu.VMEM((hb, wide, tq), BF16),
            pltpu.VMEM((hb, HEAD_DIM + VT_PAD, tq), F32),
            pltpu.VMEM((hb, 1, tq), F32),
        ],
        compiler_params=_params(3),
        name="moba",
    )(qt, km, ka, vt)


def _wo_residual_kernel(o_ref, w_ref, x_ref, out_ref):
    out_ref[...] = x_ref[...] + jnp.dot(o_ref[...], w_ref[...], preferred_element_type=F32)


def _wo_residual(o, w, x, *, tm, tn):
    m, d = x.shape
    return pl.pallas_call(
        _wo_residual_kernel,
        grid=(m // tm, d // tn),
        in_specs=[
            pl.BlockSpec((tm, d), lambda i, j: (i, 0)),
            pl.BlockSpec((d, tn), lambda i, j: (0, j)),
            pl.BlockSpec((tm, tn), lambda i, j: (i, j)),
        ],
        out_specs=pl.BlockSpec((tm, tn), lambda i, j: (i, j)),
        out_shape=jax.ShapeDtypeStruct((m, d), F32),
        compiler_params=_params(2),
        name="wo_residual",
    )(o, w, x)


def _tiles(seq, d, f):
    def pick(n, cands):
        for c in cands:
            if n % c == 0:
                return c
        raise ValueError(f"no tile for {n}")
    return dict(
        tm=pick(seq, (512, 256)),
        tm_conv=pick(seq, (256,)),
        tn=pick(d, (512, 256, 128)),
        tf=pick(f, (512, 256, 128)),
    )


def kernel(x, conv_norm_g, conv_w_pw1, conv_b_pw1, conv_w_dw, conv_b_dw, conv_ln_g, conv_ln_b,
           conv_w_pw2, conv_b_pw2, kv_norm_g, w_kv, k_norm_g, attn_norm_g, w_q, q_norm_g, w_o,
           ffn_norm_g, ffn_w_in, ffn_w_conv, ffn_w_out):
    batch, seq, d = x.shape
    depth = ffn_w_in.shape[0]
    n_a = conv_w_pw1.shape[0]
    f = ffn_w_out.shape[1]
    t = _tiles(seq, d, f)
    n_heads = d // HEAD_DIM
    nb = seq // MOBA_BLOCK
    row = lambda v: v.reshape(1, -1)

    xf = x.reshape(batch * seq, d)
    kv = None
    for layer in range(depth):
        if layer < n_a:
            a = layer
            u = _pw1_glu(xf, row(conv_norm_g[a]), conv_w_pw1[a].astype(BF16), row(conv_b_pw1[a]),
                         tm=t["tm"], tn=t["tn"])
            xf = _conv_ln_pw2(u, xf, conv_w_dw[a], row(conv_b_dw[a]), row(conv_ln_g[a]),
                              row(conv_ln_b[a]), conv_w_pw2[a].astype(BF16), row(conv_b_pw2[a]),
                              tm=t["tm_conv"], seq=seq)
        else:
            b = layer - n_a
            qt, ka, vt, km = _qkv(xf, row(kv_norm_g), row(attn_norm_g[b]), w_kv.astype(BF16),
                                  w_q[b].astype(BF16), row(k_norm_g), row(q_norm_g[b]),
                                  tm=t["tm"], tn=t["tn"], batch=batch, seq=seq)
            if kv is None:
                km = km.transpose(0, 2, 1, 3, 4).reshape(batch, n_heads, nb, HEAD_DIM)
                km = jnp.pad(km, ((0, 0), (0, 0), (0, -nb % BF16_SUBLANES), (0, 0)))
                kv = (ka, vt, km)
            ka, vt, km = kv
            o = _moba(qt, km, ka, vt, seq=seq)
            xf = _wo_residual(o, w_o[b].astype(BF16), xf, tm=t["tm"], tn=d)
        xf = _conv_ffn(xf, row(ffn_norm_g[layer]), ffn_w_in[layer].astype(BF16), ffn_w_conv[layer],
                       ffn_w_out[layer].astype(BF16), tm=t["tm"], tf=t["tf"], tn=t["tn"], seq=seq,
                       chunks=FFN_CHUNKS[layer % len(FFN_CHUNKS)])
    return xf.reshape(batch, seq, d)
```

```python
import functools

import jax
import jax.numpy as jnp
from jax import lax
from jax.experimental import pallas as pl
from jax.experimental.pallas import tpu as pltpu

EPS = 1e-6
NEG_INF = -1e30
HEAD_DIM = 128
MOBA_BLOCK = 256
TOP_K = 3
LANES = 128
SUBLANES = 8
BF16_SUBLANES = 16
VT_PAD = BF16_SUBLANES
ALIBI_TERMS = 3
VMEM_LIMIT_BYTES = 56 * 1024 * 1024

F32 = jnp.float32
BF16 = jnp.bfloat16


def _padded_blocks(seq):
    nb = seq // MOBA_BLOCK
    return -(-nb // BF16_SUBLANES) * BF16_SUBLANES


def _params(n_axes):
    return pltpu.CompilerParams(dimension_semantics=("arbitrary",) * n_axes,
                                vmem_limit_bytes=VMEM_LIMIT_BYTES)


def _rms_normalize(x):
    return x * lax.rsqrt(jnp.mean(x * x, axis=-1, keepdims=True) + EPS)


def _silu(x):
    return x * jax.nn.sigmoid(x)


def _pw1_glu_kernel(x_ref, g_ref, wa_ref, wg_ref, ba_ref, bg_ref, u_ref, h_scr):
    @pl.when(pl.program_id(1) == 0)
    def _():
        h_scr[...] = (_rms_normalize(x_ref[...]) * g_ref[...]).astype(BF16)

    h = h_scr[...]
    a = jnp.dot(h, wa_ref[...], preferred_element_type=F32) + ba_ref[...]
    gate = jnp.dot(h, wg_ref[...], preferred_element_type=F32) + bg_ref[...]
    u_ref[...] = a * jax.nn.sigmoid(gate)


def _pw1_glu(x, gain, w1, b1, *, tm, tn):
    m, d = x.shape
    nj = d // tn
    return pl.pallas_call(
        _pw1_glu_kernel,
        grid=(m // tm, nj),
        in_specs=[
            pl.BlockSpec((tm, d), lambda i, j: (i, 0)),
            pl.BlockSpec((1, d), lambda i, j: (0, 0)),
            pl.BlockSpec((d, tn), lambda i, j: (0, j)),
            pl.BlockSpec((d, tn), lambda i, j: (0, nj + j)),
            pl.BlockSpec((1, tn), lambda i, j: (0, j)),
            pl.BlockSpec((1, tn), lambda i, j: (0, nj + j)),
        ],
        out_specs=pl.BlockSpec((tm, tn), lambda i, j: (i, j)),
        out_shape=jax.ShapeDtypeStruct((m, d), F32),
        scratch_shapes=[pltpu.VMEM((tm, d), BF16)],
        compiler_params=_params(2),
        name="pw1_glu",
    )(x, gain, w1, w1, b1, b1)


CONV_HALO = 32
CONV_ROWS = 64


def _conv_ln_pw2_kernel(u_ref, uh_ref, x_ref, wdw_ref, bdw_ref, lng_ref, lnb_ref, w2_ref, b2_ref,
                        o_ref, ub_scr, y_scr, *, tm, seq, width):
    i = pl.program_id(0)
    d = u_ref.shape[1]
    first = (i * tm) % seq == 0

    @pl.when(first)
    def _():
        ub_scr[0:CONV_HALO, :] = jnp.zeros((CONV_HALO, d), F32)

    @pl.when(jnp.logical_not(first))
    def _():
        ub_scr[0:CONV_HALO, :] = uh_ref[...]

    ub_scr[CONV_HALO:CONV_HALO + tm, :] = u_ref[...]

    off0 = CONV_HALO - (width - 1)
    max_off = off0 + width - 1
    n_a = max_off // SUBLANES + 1
    z_rows = CONV_ROWS + SUBLANES
    win_rows = z_rows + (n_a - 1) * SUBLANES

    def strip(c, carry):
        col = pl.multiple_of(c * LANES, LANES)
        for r in range(tm // CONV_ROWS):
            base = r * CONV_ROWS
            rows = min(win_rows, CONV_HALO + tm - base)
            win = ub_scr[pl.ds(base, rows), pl.ds(col, LANES)]
            y = None
            for b in range(SUBLANES):
                z = None
                for a in range(n_a):
                    k = SUBLANES * a + b - off0
                    if 0 <= k < width:
                        lo = SUBLANES * a
                        hi = min(lo + z_rows, rows)
                        term = wdw_ref[pl.ds(k, 1), pl.ds(col, LANES)] * win[lo:hi, :]
                        if hi - lo < z_rows:
                            term = jnp.concatenate(
                                [term, jnp.zeros((z_rows - (hi - lo), LANES), F32)], axis=0)
                        z = term if z is None else z + term
                part = z[b:b + CONV_ROWS, :]
                y = part if y is None else y + part
            y_scr[pl.ds(base, CONV_ROWS), pl.ds(col, LANES)] = y
        return carry

    lax.fori_loop(0, d // LANES, strip, 0)

    y = y_scr[...] + bdw_ref[...]
    mu = jnp.mean(y, axis=-1, keepdims=True)
    yc = y - mu
    var = jnp.mean(yc * yc, axis=-1, keepdims=True)
    yn = yc * lax.rsqrt(var + EPS) * lng_ref[...] + lnb_ref[...]
    act = _silu(yn).astype(BF16)
    o_ref[...] = x_ref[...] + (jnp.dot(act, w2_ref[...], preferred_element_type=F32) + b2_ref[...])


def _conv_ln_pw2(u, x, w_dw, b_dw, ln_g, ln_b, w2, b2, *, tm, seq):
    m, d = u.shape
    width = w_dw.shape[0]
    assert width - 1 <= CONV_HALO and tm % CONV_ROWS == 0 and tm % CONV_HALO == 0
    hb = tm // CONV_HALO
    kern = functools.partial(_conv_ln_pw2_kernel, tm=tm, seq=seq, width=width)
    row = pl.BlockSpec((1, d), lambda i: (0, 0))
    return pl.pallas_call(
        kern,
        grid=(m // tm,),
        in_specs=[
            pl.BlockSpec((tm, d), lambda i: (i, 0)),
            pl.BlockSpec((CONV_HALO, d), lambda i: (jnp.maximum(i * hb - 1, 0), 0)),
            pl.BlockSpec((tm, d), lambda i: (i, 0)),
            pl.BlockSpec((width, d), lambda i: (0, 0)),
            row, row, row,
            pl.BlockSpec((d, d), lambda i: (0, 0)),
            row,
        ],
        out_specs=pl.BlockSpec((tm, d), lambda i: (i, 0)),
        out_shape=jax.ShapeDtypeStruct((m, d), F32),
        scratch_shapes=[pltpu.VMEM((CONV_HALO + tm, d), F32), pltpu.VMEM((tm, d), F32)],
        compiler_params=_params(1),
        name="conv_ln_pw2",
    )(u, u, x, w_dw, b_dw, ln_g, ln_b, w2, b2)


def _conv_ffn_kernel(x_ref, g_ref, wg_ref, wu_ref, wc_ref, wo_ref, o_ref,
                     h_scr, act_scr, g_scr, u_scr, tail_scr, *, tm, seq, nj, tn):
    i = pl.program_id(0)
    j = pl.program_id(1)
    tf = wg_ref.shape[1]
    first = (i * tm) % seq == 0

    def project(slot):
        h = h_scr[...]
        g_scr[slot, SUBLANES:SUBLANES + tm, :] = jnp.dot(h, wg_ref[...], preferred_element_type=F32)
        u_scr[slot] = jnp.dot(h, wu_ref[...], preferred_element_type=F32)

    def activate(jj, slot):
        g_scr[slot, 0:SUBLANES, :] = jnp.where(first, 0.0, tail_scr[jj])
        g = g_scr[slot, pl.ds(SUBLANES, tm), :]
        g1 = g_scr[slot, pl.ds(SUBLANES - 1, tm), :]
        g2 = g_scr[slot, pl.ds(SUBLANES - 2, tm), :]
        tail_scr[jj] = g_scr[slot, pl.ds(tm, SUBLANES), :]
        gc = wc_ref[2:3, :] * g + wc_ref[1:2, :] * g1 + wc_ref[0:1, :] * g2
        act = (_silu(gc) * u_scr[slot]).astype(BF16)
        act_scr[:, pl.ds(pl.multiple_of(jj * tf, tf), tf)] = act

    @pl.when(jnp.logical_and(i == 0, j == 0))
    def _():
        tail_scr[...] = jnp.zeros(tail_scr.shape, F32)

    @pl.when(j == 0)
    def _():
        h_scr[...] = (_rms_normalize(x_ref[...]) * g_ref[...]).astype(BF16)
        project(0)

    @pl.when(jnp.logical_and(j >= 1, j < nj))
    def _():
        slot = j % 2
        project(slot)
        activate(j - 1, 1 - slot)

    @pl.when(j == nj)
    def _():
        activate(nj - 1, (nj - 1) % 2)

    @pl.when(j >= nj)
    def _():
        col = pl.multiple_of((j - nj) * tn, tn)
        o_ref[...] = x_ref[:, pl.ds(col, tn)] + jnp.dot(
            act_scr[...], wo_ref[...], preferred_element_type=F32)


def _conv_ffn(x, gain, w_in, w_conv, w_out, *, tm, tf, tn, seq):
    m, d = x.shape
    f = w_out.shape[0]
    assert w_conv.shape[0] == 3 and f % tf == 0 and d % tn == 0
    nj = f // tf
    kern = functools.partial(_conv_ffn_kernel, tm=tm, seq=seq, nj=nj, tn=tn)
    proj = lambda j: jnp.where(j < nj, j, 0)
    actv = lambda j: jnp.clip(j - 1, 0, nj - 1)
    outc = lambda j: jnp.maximum(j - nj, 0)
    wout = lambda j: jnp.where(j < nj - 1, d // tn - 1, outc(j))
    return pl.pallas_call(
        kern,
        grid=(m // tm, nj + d // tn),
        in_specs=[
            pl.BlockSpec((tm, d), lambda i, j: (i, 0)),
            pl.BlockSpec((1, d), lambda i, j: (0, 0)),
            pl.BlockSpec((d, tf), lambda i, j: (0, proj(j))),
            pl.BlockSpec((d, tf), lambda i, j: (0, nj + proj(j))),
            pl.BlockSpec((3, tf), lambda i, j: (0, actv(j))),
            pl.BlockSpec((f, tn), lambda i, j: (0, wout(j))),
        ],
        out_specs=pl.BlockSpec((tm, tn), lambda i, j: (i, outc(j))),
        out_shape=jax.ShapeDtypeStruct((m, d), F32),
        scratch_shapes=[
            pltpu.VMEM((tm, d), BF16),
            pltpu.VMEM((tm, f), BF16),
            pltpu.VMEM((2, SUBLANES + tm, tf), F32),
            pltpu.VMEM((2, tm, tf), F32),
            pltpu.VMEM((nj, SUBLANES, tf), F32),
        ],
        compiler_params=_params(2),
        name="conv_ffn",
    )(x, gain, w_in, w_in, w_conv, w_out)


def _qkv_kernel(x_ref, gkv_ref, gq_ref, wk_ref, wv_ref, wq_ref, kng_ref, qng_ref,
                qt_ref, ka_ref, vt_ref, km_ref, hkv_scr, hq_scr, *, tm, seq, hps, nbp):
    i = pl.program_id(0)

    @pl.when(pl.program_id(1) == 0)
    def _():
        xn = _rms_normalize(x_ref[...])
        hkv_scr[...] = (xn * gkv_ref[...]).astype(BF16)
        hq_scr[...] = (xn * gq_ref[...]).astype(BF16)

    nblk = tm // MOBA_BLOCK
    n_heads = x_ref.shape[1] // HEAD_DIM
    shape = (MOBA_BLOCK, LANES)
    lane = lax.broadcasted_iota(jnp.int32, shape, 1)
    ones_row = (lax.broadcasted_iota(jnp.int32, (VT_PAD, MOBA_BLOCK), 0) == 0).astype(BF16)

    for bb in range(nblk):
        rows = slice(bb * MOBA_BLOCK, (bb + 1) * MOBA_BLOCK)
        hkv = hkv_scr[rows, :]
        kk = jnp.dot(hkv, wk_ref[...], preferred_element_type=F32)
        vv = jnp.dot(hkv, wv_ref[...], preferred_element_type=F32)
        qq = jnp.dot(hq_scr[rows, :], wq_ref[...], preferred_element_type=F32)

        row0 = (i * tm) % seq + bb * MOBA_BLOCK
        onehot = (lane == row0 // MOBA_BLOCK).astype(BF16)
        pos = (row0 + lax.broadcasted_iota(jnp.int32, shape, 0)).astype(F32)

        for hh in range(hps):
            sl = slice(hh * HEAD_DIM, (hh + 1) * HEAD_DIM)
            kn = _rms_normalize(kk[:, sl]) * kng_ref[...]
            qt_ref[0, hh, :, rows] = (_rms_normalize(qq[:, sl]) * qng_ref[...]).T
            ka_ref[0, hh, rows, 0:HEAD_DIM] = kn.astype(BF16)
            hv = (jnp.zeros(shape, jnp.int32) + (pl.program_id(1) * hps + hh + 1)).astype(F32)
            a = jnp.exp2(hv * (-8.0 / n_heads)) * pos
            a_hi = a.astype(BF16)
            r1 = a - a_hi.astype(F32)
            a_mid = r1.astype(BF16)
            a_lo = (r1 - a_mid.astype(F32)).astype(BF16)
            ka_ref[0, hh, rows, HEAD_DIM:HEAD_DIM + LANES] = jnp.where(
                lane == nbp, a_hi, jnp.where(lane == nbp + 1, a_mid,
                                             jnp.where(lane == nbp + 2, a_lo, onehot)))
            vt_ref[0, hh, bb, 0:HEAD_DIM, :] = vv[:, sl].T.astype(BF16)
            vt_ref[0, hh, bb, HEAD_DIM:HEAD_DIM + VT_PAD, :] = ones_row
            km_ref[0, 0, hh, bb:bb + 1, :] = jnp.mean(kn, axis=0, keepdims=True)


def _qkv(x, g_kv, g_q, w_kv, w_q, k_norm_g, q_norm_g, *, tm, tn, batch, seq):
    m, d = x.shape
    assert tn % HEAD_DIM == 0 and tm % MOBA_BLOCK == 0 and seq % tm == 0
    nj = d // tn
    hps = tn // HEAD_DIM
    n_heads = d // HEAD_DIM
    tps = seq // tm
    nblk = tm // MOBA_BLOCK
    nbp = _padded_blocks(seq)
    assert nbp + ALIBI_TERMS <= LANES
    kern = functools.partial(_qkv_kernel, tm=tm, seq=seq, hps=hps, nbp=nbp)
    row_d = pl.BlockSpec((1, d), lambda i, j: (0, 0))
    row_h = pl.BlockSpec((1, HEAD_DIM), lambda i, j: (0, 0))
    return pl.pallas_call(
        kern,
        grid=(m // tm, nj),
        in_specs=[
            pl.BlockSpec((tm, d), lambda i, j: (i, 0)),
            row_d, row_d,
            pl.BlockSpec((d, tn), lambda i, j: (0, j)),
            pl.BlockSpec((d, tn), lambda i, j: (0, nj + j)),
            pl.BlockSpec((d, tn), lambda i, j: (0, j)),
            row_h, row_h,
        ],
        out_specs=[
            pl.BlockSpec((1, hps, HEAD_DIM, tm), lambda i, j: (i // tps, j, 0, i % tps)),
            pl.BlockSpec((1, hps, tm, HEAD_DIM + LANES), lambda i, j: (i // tps, j, i % tps, 0)),
            pl.BlockSpec((1, hps, nblk, HEAD_DIM + VT_PAD, MOBA_BLOCK),
                         lambda i, j: (i // tps, j, i % tps, 0, 0)),
            pl.BlockSpec((1, 1, hps, nblk, HEAD_DIM), lambda i, j: (i // tps, i % tps, j, 0, 0)),
        ],
        out_shape=[
            jax.ShapeDtypeStruct((batch, n_heads, HEAD_DIM, seq), F32),
            jax.ShapeDtypeStruct((batch, n_heads, seq, HEAD_DIM + LANES), BF16),
            jax.ShapeDtypeStruct((batch, n_heads, seq // MOBA_BLOCK, HEAD_DIM + VT_PAD, MOBA_BLOCK),
                                 BF16),
            jax.ShapeDtypeStruct((batch, tps, n_heads, nblk, HEAD_DIM), F32),
        ],
        scratch_shapes=[pltpu.VMEM((tm, d), BF16), pltpu.VMEM((tm, d), BF16)],
        compiler_params=_params(2),
        name="qkv",
    )(x, g_kv, g_q, w_kv, w_kv, w_q, k_norm_g, q_norm_g)


MOBA_HEADS = 8


def _moba_kernel(qt_ref, km_ref, ka_ref, vt_ref, o_ref, qat_scr, acc_scr, m_scr, *, n_heads, hb):
    hg = pl.program_id(1)
    i = pl.program_id(2)
    tq = MOBA_BLOCK
    nbp = km_ref.shape[2]
    heads = range(hb)

    blk = lax.broadcasted_iota(jnp.int32, (nbp, tq), 0)
    blk_f = blk.astype(F32)
    past = blk < i
    rest_rows = LANES - nbp
    ones_rows = (lax.broadcasted_iota(jnp.int32, (rest_rows, tq), 0) < ALIBI_TERMS).astype(BF16)
    for hh in heads:
        qt = qt_ref[0, hh]
        gate = jnp.dot(km_ref[0, hh], qt, precision=lax.Precision.HIGHEST,
                       preferred_element_type=F32)
        work = jnp.where(past, gate, NEG_INF)
        sel = jnp.zeros(gate.shape, jnp.bool_)
        for _ in range(TOP_K):
            mx = jnp.max(work, axis=0, keepdims=True)
            idx = jnp.min(jnp.where(work == mx, blk_f, float(LANES)), axis=0, keepdims=True)
            hit = blk_f == idx
            sel = jnp.logical_or(sel, hit)
            work = jnp.where(hit, -jnp.inf, work)
        allowed = jnp.logical_or(jnp.logical_and(sel, past), blk == i)
        qat_scr[hh, 0:HEAD_DIM, :] = (qt * HEAD_DIM ** -0.5).astype(BF16)
        qat_scr[hh, HEAD_DIM:HEAD_DIM + nbp, :] = jnp.where(allowed, 0.0, NEG_INF).astype(BF16)
        qat_scr[hh, HEAD_DIM + nbp:HEAD_DIM + LANES, :] = ones_rows

    t_q = (i * tq + lax.broadcasted_iota(jnp.int32, (1, tq), 1)).astype(F32)
    b = []
    for hh in heads:
        hv = (jnp.zeros((1, tq), jnp.int32) + (hg * hb + hh + 1)).astype(F32)
        b.append(jnp.exp2(hv * (-8.0 / n_heads)) * t_q)

    def scores(n, hh):
        start = pl.multiple_of(n * tq, tq)
        return jnp.dot(ka_ref[0, hh, pl.ds(start, tq), :], qat_scr[hh],
                       preferred_element_type=F32)

    def key_max(s):
        return jnp.max(s, axis=0, keepdims=True)

    odd = i % 2
    near = jnp.maximum(i - 1, 0)
    near_bias = jnp.where(odd == 1, 0.0, NEG_INF)
    causal = (lax.broadcasted_iota(jnp.int32, (tq, tq), 1)
              >= lax.broadcasted_iota(jnp.int32, (tq, tq), 0))
    s0 = [jnp.where(causal, scores(i, hh), NEG_INF) for hh in heads]
    s1 = [scores(near, hh) + near_bias for hh in heads]
    m0 = [jnp.maximum(key_max(s0[hh]), key_max(s1[hh])) - b[hh] for hh in heads]
    p0 = [jnp.exp(s0[hh] - (m0[hh] + b[hh])).astype(BF16) for hh in heads]
    p1 = [jnp.exp(s1[hh] - (m0[hh] + b[hh])).astype(BF16) for hh in heads]
    for hh in heads:
        acc_scr[hh] = (jnp.dot(vt_ref[0, hh, i], p0[hh], preferred_element_type=F32)
                       + jnp.dot(vt_ref[0, hh, near], p1[hh], preferred_element_type=F32))
        m_scr[hh] = m0[hh]

    rest = i - odd

    def pair(t, carry):
        na = rest - 1 - 2 * t
        nb = na - 1
        sa = [scores(na, hh) for hh in heads]
        sb = [scores(nb, hh) for hh in heads]
        m_prev = [m_scr[hh] for hh in heads]
        m_new = [jnp.maximum(m_prev[hh], jnp.maximum(key_max(sa[hh]), key_max(sb[hh])) - b[hh])
                 for hh in heads]
        alpha = [jnp.exp(m_prev[hh] - m_new[hh]) for hh in heads]
        pa = [jnp.exp(sa[hh] - (m_new[hh] + b[hh])).astype(BF16) for hh in heads]
        pb = [jnp.exp(sb[hh] - (m_new[hh] + b[hh])).astype(BF16) for hh in heads]
        for hh in heads:
            acc_scr[hh] = (alpha[hh] * acc_scr[hh]
                           + jnp.dot(vt_ref[0, hh, na], pa[hh], preferred_element_type=F32)
                           + jnp.dot(vt_ref[0, hh, nb], pb[hh], preferred_element_type=F32))
            m_scr[hh] = m_new[hh]
        return carry

    lax.fori_loop(0, rest // 2, pair, 0)
    for hh in heads:
        acc = acc_scr[hh]
        ot = acc[0:HEAD_DIM, :] / acc[HEAD_DIM:HEAD_DIM + 1, :]
        o_ref[:, hh * HEAD_DIM:(hh + 1) * HEAD_DIM] = ot.T.astype(o_ref.dtype)


def _moba(qt, km, ka, vt, *, seq):
    batch, n_heads, _, _ = qt.shape
    d = n_heads * HEAD_DIM
    nb = seq // MOBA_BLOCK
    nbp = km.shape[2]
    assert nbp == _padded_blocks(seq) and nbp + ALIBI_TERMS <= LANES
    hb = MOBA_HEADS if n_heads % MOBA_HEADS == 0 else 1
    tq = MOBA_BLOCK
    wide = HEAD_DIM + LANES
    kern = functools.partial(_moba_kernel, n_heads=n_heads, hb=hb)
    return pl.pallas_call(
        kern,
        grid=(batch, n_heads // hb, nb),
        in_specs=[
            pl.BlockSpec((1, hb, HEAD_DIM, tq), lambda b, g, i: (b, g, 0, i)),
            pl.BlockSpec((1, hb, nbp, HEAD_DIM), lambda b, g, i: (b, g, 0, 0)),
            pl.BlockSpec((1, hb, seq, wide), lambda b, g, i: (b, g, 0, 0)),
            pl.BlockSpec((1, hb, nb, HEAD_DIM + VT_PAD, tq), lambda b, g, i: (b, g, 0, 0, 0),
                         pipeline_mode=pl.Buffered(1)),
        ],
        out_specs=pl.BlockSpec((tq, hb * HEAD_DIM), lambda b, g, i: (b * nb + i, g)),
        out_shape=jax.ShapeDtypeStruct((batch * seq, d), BF16),
        scratch_shapes=[
            pltpu.VMEM((hb, wide, tq), BF16),
            pltpu.VMEM((hb, HEAD_DIM + VT_PAD, tq), F32),
            pltpu.VMEM((hb, 1, tq), F32),
        ],
        compiler_params=_params(3),
        name="moba",
    )(qt, km, ka, vt)


def _wo_residual_kernel(o_ref, w_ref, x_ref, out_ref):
    out_ref[...] = x_ref[...] + jnp.dot(o_ref[...], w_ref[...], preferred_element_type=F32)


def _wo_residual(o, w, x, *, tm, tn):
    m, d = x.shape
    return pl.pallas_call(
        _wo_residual_kernel,
        grid=(m // tm, d // tn),
        in_specs=[
            pl.BlockSpec((tm, d), lambda i, j: (i, 0)),
            pl.BlockSpec((d, tn), lambda i, j: (0, j)),
            pl.BlockSpec((tm, tn), lambda i, j: (i, j)),
        ],
        out_specs=pl.BlockSpec((tm, tn), lambda i, j: (i, j)),
        out_shape=jax.ShapeDtypeStruct((m, d), F32),
        compiler_params=_params(2),
        name="wo_residual",
    )(o, w, x)


def _tiles(seq, d, f):
    def pick(n, cands):
        for c in cands:
            if n % c == 0:
                return c
        raise ValueError(f"no tile for {n}")
    return dict(
        tm=pick(seq, (512, 256)),
        tm_conv=pick(seq, (256,)),
        tn=pick(d, (512, 256, 128)),
        tf=pick(f, (512, 256, 128)),
    )


def kernel(x, conv_norm_g, conv_w_pw1, conv_b_pw1, conv_w_dw, conv_b_dw, conv_ln_g, conv_ln_b,
           conv_w_pw2, conv_b_pw2, kv_norm_g, w_kv, k_norm_g, attn_norm_g, w_q, q_norm_g, w_o,
           ffn_norm_g, ffn_w_in, ffn_w_conv, ffn_w_out):
    batch, seq, d = x.shape
    depth = ffn_w_in.shape[0]
    n_a = conv_w_pw1.shape[0]
    f = ffn_w_out.shape[1]
    t = _tiles(seq, d, f)
    n_heads = d // HEAD_DIM
    nb = seq // MOBA_BLOCK
    row = lambda v: v.reshape(1, -1)

    xf = x.reshape(batch * seq, d)
    kv = None
    for layer in range(depth):
        if layer < n_a:
            a = layer
            u = _pw1_glu(xf, row(conv_norm_g[a]), conv_w_pw1[a].astype(BF16), row(conv_b_pw1[a]),
                         tm=t["tm"], tn=t["tn"])
            xf = _conv_ln_pw2(u, xf, conv_w_dw[a], row(conv_b_dw[a]), row(conv_ln_g[a]),
                              row(conv_ln_b[a]), conv_w_pw2[a].astype(BF16), row(conv_b_pw2[a]),
                              tm=t["tm_conv"], seq=seq)
        else:
            b = layer - n_a
            qt, ka, vt, km = _qkv(xf, row(kv_norm_g), row(attn_norm_g[b]), w_kv.astype(BF16),
                                  w_q[b].astype(BF16), row(k_norm_g), row(q_norm_g[b]),
                                  tm=t["tm"], tn=t["tn"], batch=batch, seq=seq)
            if kv is None:
                km = km.transpose(0, 2, 1, 3, 4).reshape(batch, n_heads, nb, HEAD_DIM)
                km = jnp.pad(km, ((0, 0), (0, 0), (0, -nb % BF16_SUBLANES), (0, 0)))
                kv = (ka, vt, km)
            ka, vt, km = kv
            o = _moba(qt, km, ka, vt, seq=seq)
            xf = _wo_residual(o, w_o[b].astype(BF16), xf, tm=t["tm"], tn=d)
        xf = _conv_ffn(xf, row(ffn_norm_g[layer]), ffn_w_in[layer].astype(BF16), ffn_w_conv[layer],
                       ffn_w_out[layer].astype(BF16), tm=t["tm"], tf=t["tf"], tn=t["tn"], seq=seq)
    return xf.reshape(batch, seq, d)
```

```python
import functools

import jax
import jax.numpy as jnp
from jax import lax
from jax.experimental import pallas as pl
from jax.experimental.pallas import tpu as pltpu

EPS = 1e-6
NEG_INF = -1e30
HEAD_DIM = 128
MOBA_BLOCK = 256
TOP_K = 3
LANES = 128
SUBLANES = 8
BF16_SUBLANES = 16
VT_PAD = BF16_SUBLANES
ALIBI_TERMS = 3
VMEM_LIMIT_BYTES = 56 * 1024 * 1024

F32 = jnp.float32
BF16 = jnp.bfloat16


def _padded_blocks(seq):
    nb = seq // MOBA_BLOCK
    return -(-nb // BF16_SUBLANES) * BF16_SUBLANES


def _params(n_axes):
    return pltpu.CompilerParams(dimension_semantics=("arbitrary",) * n_axes,
                                vmem_limit_bytes=VMEM_LIMIT_BYTES)


def _rms_normalize(x):
    return x * lax.rsqrt(jnp.mean(x * x, axis=-1, keepdims=True) + EPS)


def _silu(x):
    return x * jax.nn.sigmoid(x)


def _pw1_glu_kernel(x_ref, g_ref, wa_ref, wg_ref, ba_ref, bg_ref, u_ref, h_scr):
    @pl.when(pl.program_id(1) == 0)
    def _():
        h_scr[...] = (_rms_normalize(x_ref[...]) * g_ref[...]).astype(BF16)

    h = h_scr[...]
    a = jnp.dot(h, wa_ref[...], preferred_element_type=F32) + ba_ref[...]
    gate = jnp.dot(h, wg_ref[...], preferred_element_type=F32) + bg_ref[...]
    u_ref[...] = a * jax.nn.sigmoid(gate)


def _pw1_glu(x, gain, w1, b1, *, tm, tn):
    m, d = x.shape
    nj = d // tn
    return pl.pallas_call(
        _pw1_glu_kernel,
        grid=(m // tm, nj),
        in_specs=[
            pl.BlockSpec((tm, d), lambda i, j: (i, 0)),
            pl.BlockSpec((1, d), lambda i, j: (0, 0)),
            pl.BlockSpec((d, tn), lambda i, j: (0, j)),
            pl.BlockSpec((d, tn), lambda i, j: (0, nj + j)),
            pl.BlockSpec((1, tn), lambda i, j: (0, j)),
            pl.BlockSpec((1, tn), lambda i, j: (0, nj + j)),
        ],
        out_specs=pl.BlockSpec((tm, tn), lambda i, j: (i, j)),
        out_shape=jax.ShapeDtypeStruct((m, d), F32),
        scratch_shapes=[pltpu.VMEM((tm, d), BF16)],
        compiler_params=_params(2),
        name="pw1_glu",
    )(x, gain, w1, w1, b1, b1)


CONV_HALO = 32
CONV_ROWS = 64


def _conv_ln_pw2_kernel(u_ref, uh_ref, x_ref, wdw_ref, bdw_ref, lng_ref, lnb_ref, w2_ref, b2_ref,
                        o_ref, ub_scr, y_scr, *, tm, seq, width):
    i = pl.program_id(0)
    d = u_ref.shape[1]
    first = (i * tm) % seq == 0

    @pl.when(first)
    def _():
        ub_scr[0:CONV_HALO, :] = jnp.zeros((CONV_HALO, d), F32)

    @pl.when(jnp.logical_not(first))
    def _():
        ub_scr[0:CONV_HALO, :] = uh_ref[...]

    ub_scr[CONV_HALO:CONV_HALO + tm, :] = u_ref[...]

    off0 = CONV_HALO - (width - 1)
    max_off = off0 + width - 1
    n_a = max_off // SUBLANES + 1
    z_rows = CONV_ROWS + SUBLANES
    win_rows = z_rows + (n_a - 1) * SUBLANES

    def strip(c, carry):
        col = pl.multiple_of(c * LANES, LANES)
        for r in range(tm // CONV_ROWS):
            base = r * CONV_ROWS
            rows = min(win_rows, CONV_HALO + tm - base)
            win = ub_scr[pl.ds(base, rows), pl.ds(col, LANES)]
            y = None
            for b in range(SUBLANES):
                z = None
                for a in range(n_a):
                    k = SUBLANES * a + b - off0
                    if 0 <= k < width:
                        lo = SUBLANES * a
                        hi = min(lo + z_rows, rows)
                        term = wdw_ref[pl.ds(k, 1), pl.ds(col, LANES)] * win[lo:hi, :]
                        if hi - lo < z_rows:
                            term = jnp.concatenate(
                                [term, jnp.zeros((z_rows - (hi - lo), LANES), F32)], axis=0)
                        z = term if z is None else z + term
                part = z[b:b + CONV_ROWS, :]
                y = part if y is None else y + part
            y_scr[pl.ds(base, CONV_ROWS), pl.ds(col, LANES)] = y
        return carry

    lax.fori_loop(0, d // LANES, strip, 0)

    y = y_scr[...] + bdw_ref[...]
    mu = jnp.mean(y, axis=-1, keepdims=True)
    yc = y - mu
    var = jnp.mean(yc * yc, axis=-1, keepdims=True)
    yn = yc * lax.rsqrt(var + EPS) * lng_ref[...] + lnb_ref[...]
    act = _silu(yn).astype(BF16)
    o_ref[...] = x_ref[...] + (jnp.dot(act, w2_ref[...], preferred_element_type=F32) + b2_ref[...])


def _conv_ln_pw2(u, x, w_dw, b_dw, ln_g, ln_b, w2, b2, *, tm, seq):
    m, d = u.shape
    width = w_dw.shape[0]
    assert width - 1 <= CONV_HALO and tm % CONV_ROWS == 0 and tm % CONV_HALO == 0
    hb = tm // CONV_HALO
    kern = functools.partial(_conv_ln_pw2_kernel, tm=tm, seq=seq, width=width)
    row = pl.BlockSpec((1, d), lambda i: (0, 0))
    return pl.pallas_call(
        kern,
        grid=(m // tm,),
        in_specs=[
            pl.BlockSpec((tm, d), lambda i: (i, 0)),
            pl.BlockSpec((CONV_HALO, d), lambda i: (jnp.maximum(i * hb - 1, 0), 0)),
            pl.BlockSpec((tm, d), lambda i: (i, 0)),
            pl.BlockSpec((width, d), lambda i: (0, 0)),
            row, row, row,
            pl.BlockSpec((d, d), lambda i: (0, 0)),
            row,
        ],
        out_specs=pl.BlockSpec((tm, d), lambda i: (i, 0)),
        out_shape=jax.ShapeDtypeStruct((m, d), F32),
        scratch_shapes=[pltpu.VMEM((CONV_HALO + tm, d), F32), pltpu.VMEM((tm, d), F32)],
        compiler_params=_params(1),
        name="conv_ln_pw2",
    )(u, u, x, w_dw, b_dw, ln_g, ln_b, w2, b2)


def _conv_ffn_kernel(x_ref, g_ref, wg_ref, wu_ref, wc_ref, wo_ref, o_ref,
                     h_scr, act_scr, g_scr, u_scr, tail_scr, *, tm, seq, nj, tn):
    i = pl.program_id(0)
    j = pl.program_id(1)
    tf = wg_ref.shape[1]
    first = (i * tm) % seq == 0

    def project(slot):
        h = h_scr[...]
        g_scr[slot, SUBLANES:SUBLANES + tm, :] = jnp.dot(h, wg_ref[...], preferred_element_type=F32)
        u_scr[slot] = jnp.dot(h, wu_ref[...], preferred_element_type=F32)

    def activate(jj, slot):
        g_scr[slot, 0:SUBLANES, :] = jnp.where(first, 0.0, tail_scr[jj])
        g = g_scr[slot, pl.ds(SUBLANES, tm), :]
        g1 = g_scr[slot, pl.ds(SUBLANES - 1, tm), :]
        g2 = g_scr[slot, pl.ds(SUBLANES - 2, tm), :]
        tail_scr[jj] = g_scr[slot, pl.ds(tm, SUBLANES), :]
        gc = wc_ref[2:3, :] * g + wc_ref[1:2, :] * g1 + wc_ref[0:1, :] * g2
        act = (_silu(gc) * u_scr[slot]).astype(BF16)
        act_scr[:, pl.ds(pl.multiple_of(jj * tf, tf), tf)] = act

    @pl.when(jnp.logical_and(i == 0, j == 0))
    def _():
        tail_scr[...] = jnp.zeros(tail_scr.shape, F32)

    @pl.when(j == 0)
    def _():
        h_scr[...] = (_rms_normalize(x_ref[...]) * g_ref[...]).astype(BF16)
        project(0)

    @pl.when(jnp.logical_and(j >= 1, j < nj))
    def _():
        slot = j % 2
        project(slot)
        activate(j - 1, 1 - slot)

    @pl.when(j == nj)
    def _():
        activate(nj - 1, (nj - 1) % 2)

    @pl.when(j >= nj)
    def _():
        col = pl.multiple_of((j - nj) * tn, tn)
        o_ref[...] = x_ref[:, pl.ds(col, tn)] + jnp.dot(
            act_scr[...], wo_ref[...], preferred_element_type=F32)


def _conv_ffn(x, gain, w_in, w_conv, w_out, *, tm, tf, tn, seq):
    m, d = x.shape
    f = w_out.shape[0]
    assert w_conv.shape[0] == 3 and f % tf == 0 and d % tn == 0
    nj = f // tf
    kern = functools.partial(_conv_ffn_kernel, tm=tm, seq=seq, nj=nj, tn=tn)
    proj = lambda j: jnp.where(j < nj, j, 0)
    actv = lambda j: jnp.clip(j - 1, 0, nj - 1)
    outc = lambda j: jnp.maximum(j - nj, 0)
    wout = lambda j: jnp.where(j < nj - 1, d // tn - 1, outc(j))
    return pl.pallas_call(
        kern,
        grid=(m // tm, nj + d // tn),
        in_specs=[
            pl.BlockSpec((tm, d), lambda i, j: (i, 0)),
            pl.BlockSpec((1, d), lambda i, j: (0, 0)),
            pl.BlockSpec((d, tf), lambda i, j: (0, proj(j))),
            pl.BlockSpec((d, tf), lambda i, j: (0, nj + proj(j))),
            pl.BlockSpec((3, tf), lambda i, j: (0, actv(j))),
            pl.BlockSpec((f, tn), lambda i, j: (0, wout(j))),
        ],
        out_specs=pl.BlockSpec((tm, tn), lambda i, j: (i, outc(j))),
        out_shape=jax.ShapeDtypeStruct((m, d), F32),
        scratch_shapes=[
            pltpu.VMEM((tm, d), BF16),
            pltpu.VMEM((tm, f), BF16),
            pltpu.VMEM((2, SUBLANES + tm, tf), F32),
            pltpu.VMEM((2, tm, tf), F32),
            pltpu.VMEM((nj, SUBLANES, tf), F32),
        ],
        compiler_params=_params(2),
        name="conv_ffn",
    )(x, gain, w_in, w_in, w_conv, w_out)


def _serpentine(nj, i, j):
    return jnp.where(i % 2 == 0, j, nj - 1 - j)


def _qkv_kernel(x_ref, gkv_ref, gq_ref, wk_ref, wv_ref, wq_ref, kng_ref, qng_ref,
                qt_ref, ka_ref, vt_ref, km_ref, hkv_scr, hq_scr, *, tm, seq, hps, nbp, nj):
    i = pl.program_id(0)
    head_group = _serpentine(nj, i, pl.program_id(1))

    @pl.when(pl.program_id(1) == 0)
    def _():
        xn = _rms_normalize(x_ref[...])
        hkv_scr[...] = (xn * gkv_ref[...]).astype(BF16)
        hq_scr[...] = (xn * gq_ref[...]).astype(BF16)

    nblk = tm // MOBA_BLOCK
    n_heads = x_ref.shape[1] // HEAD_DIM
    shape = (MOBA_BLOCK, LANES)
    lane = lax.broadcasted_iota(jnp.int32, shape, 1)
    ones_row = (lax.broadcasted_iota(jnp.int32, (VT_PAD, MOBA_BLOCK), 0) == 0).astype(BF16)

    for bb in range(nblk):
        rows = slice(bb * MOBA_BLOCK, (bb + 1) * MOBA_BLOCK)
        hkv = hkv_scr[rows, :]
        kk = jnp.dot(hkv, wk_ref[...], preferred_element_type=F32)
        vv = jnp.dot(hkv, wv_ref[...], preferred_element_type=F32)
        qq = jnp.dot(hq_scr[rows, :], wq_ref[...], preferred_element_type=F32)

        row0 = (i * tm) % seq + bb * MOBA_BLOCK
        onehot = (lane == row0 // MOBA_BLOCK).astype(BF16)
        pos = (row0 + lax.broadcasted_iota(jnp.int32, shape, 0)).astype(F32)

        for hh in range(hps):
            sl = slice(hh * HEAD_DIM, (hh + 1) * HEAD_DIM)
            kn = _rms_normalize(kk[:, sl]) * kng_ref[...]
            qt_ref[0, hh, :, rows] = (_rms_normalize(qq[:, sl]) * qng_ref[...]).T
            ka_ref[0, hh, rows, 0:HEAD_DIM] = kn.astype(BF16)
            hv = (jnp.zeros(shape, jnp.int32) + (head_group * hps + hh + 1)).astype(F32)
            a = jnp.exp2(hv * (-8.0 / n_heads)) * pos
            a_hi = a.astype(BF16)
            r1 = a - a_hi.astype(F32)
            a_mid = r1.astype(BF16)
            a_lo = (r1 - a_mid.astype(F32)).astype(BF16)
            ka_ref[0, hh, rows, HEAD_DIM:HEAD_DIM + LANES] = jnp.where(
                lane == nbp, a_hi, jnp.where(lane == nbp + 1, a_mid,
                                             jnp.where(lane == nbp + 2, a_lo, onehot)))
            vt_ref[0, hh, bb, 0:HEAD_DIM, :] = vv[:, sl].T.astype(BF16)
            vt_ref[0, hh, bb, HEAD_DIM:HEAD_DIM + VT_PAD, :] = ones_row
            km_ref[0, 0, hh, bb:bb + 1, :] = jnp.mean(kn, axis=0, keepdims=True)


def _qkv(x, g_kv, g_q, w_kv, w_q, k_norm_g, q_norm_g, *, tm, tn, batch, seq):
    m, d = x.shape
    assert tn % HEAD_DIM == 0 and tm % MOBA_BLOCK == 0 and seq % tm == 0
    nj = d // tn
    hps = tn // HEAD_DIM
    n_heads = d // HEAD_DIM
    tps = seq // tm
    nblk = tm // MOBA_BLOCK
    nbp = _padded_blocks(seq)
    assert nbp + ALIBI_TERMS <= LANES
    kern = functools.partial(_qkv_kernel, tm=tm, seq=seq, hps=hps, nbp=nbp, nj=nj)
    col = functools.partial(_serpentine, nj)
    row_d = pl.BlockSpec((1, d), lambda i, j: (0, 0))
    row_h = pl.BlockSpec((1, HEAD_DIM), lambda i, j: (0, 0))
    return pl.pallas_call(
        kern,
        grid=(m // tm, nj),
        in_specs=[
            pl.BlockSpec((tm, d), lambda i, j: (i, 0)),
            row_d, row_d,
            pl.BlockSpec((d, tn), lambda i, j: (0, col(i, j))),
            pl.BlockSpec((d, tn), lambda i, j: (0, nj + col(i, j))),
            pl.BlockSpec((d, tn), lambda i, j: (0, col(i, j))),
            row_h, row_h,
        ],
        out_specs=[
            pl.BlockSpec((1, hps, HEAD_DIM, tm), lambda i, j: (i // tps, col(i, j), 0, i % tps)),
            pl.BlockSpec((1, hps, tm, HEAD_DIM + LANES),
                         lambda i, j: (i // tps, col(i, j), i % tps, 0)),
            pl.BlockSpec((1, hps, nblk, HEAD_DIM + VT_PAD, MOBA_BLOCK),
                         lambda i, j: (i // tps, col(i, j), i % tps, 0, 0)),
            pl.BlockSpec((1, 1, hps, nblk, HEAD_DIM),
                         lambda i, j: (i // tps, i % tps, col(i, j), 0, 0)),
        ],
        out_shape=[
            jax.ShapeDtypeStruct((batch, n_heads, HEAD_DIM, seq), F32),
            jax.ShapeDtypeStruct((batch, n_heads, seq, HEAD_DIM + LANES), BF16),
            jax.ShapeDtypeStruct((batch, n_heads, seq // MOBA_BLOCK, HEAD_DIM + VT_PAD, MOBA_BLOCK),
                                 BF16),
            jax.ShapeDtypeStruct((batch, tps, n_heads, nblk, HEAD_DIM), F32),
        ],
        scratch_shapes=[pltpu.VMEM((tm, d), BF16), pltpu.VMEM((tm, d), BF16)],
        compiler_params=_params(2),
        name="qkv",
    )(x, g_kv, g_q, w_kv, w_kv, w_q, k_norm_g, q_norm_g)


MOBA_HEADS = 8


def _moba_kernel(qt_ref, km_ref, ka_ref, vt_ref, o_ref, qat_scr, acc_scr, m_scr, *, n_heads, hb):
    hg = pl.program_id(1)
    i = pl.program_id(2)
    tq = MOBA_BLOCK
    nbp = km_ref.shape[2]
    heads = range(hb)

    blk = lax.broadcasted_iota(jnp.int32, (nbp, tq), 0)
    blk_f = blk.astype(F32)
    past = blk < i
    rest_rows = LANES - nbp
    ones_rows = (lax.broadcasted_iota(jnp.int32, (rest_rows, tq), 0) < ALIBI_TERMS).astype(BF16)
    for hh in heads:
        qt = qt_ref[0, hh]
        gate = jnp.dot(km_ref[0, hh], qt, precision=lax.Precision.HIGHEST,
                       preferred_element_type=F32)
        work = jnp.where(past, gate, NEG_INF)
        sel = jnp.zeros(gate.shape, jnp.bool_)
        for _ in range(TOP_K):
            mx = jnp.max(work, axis=0, keepdims=True)
            idx = jnp.min(jnp.where(work == mx, blk_f, float(LANES)), axis=0, keepdims=True)
            hit = blk_f == idx
            sel = jnp.logical_or(sel, hit)
            work = jnp.where(hit, -jnp.inf, work)
        allowed = jnp.logical_or(jnp.logical_and(sel, past), blk == i)
        qat_scr[hh, 0:HEAD_DIM, :] = (qt * HEAD_DIM ** -0.5).astype(BF16)
        qat_scr[hh, HEAD_DIM:HEAD_DIM + nbp, :] = jnp.where(allowed, 0.0, NEG_INF).astype(BF16)
        qat_scr[hh, HEAD_DIM + nbp:HEAD_DIM + LANES, :] = ones_rows

    t_q = (i * tq + lax.broadcasted_iota(jnp.int32, (1, tq), 1)).astype(F32)
    b = []
    for hh in heads:
        hv = (jnp.zeros((1, tq), jnp.int32) + (hg * hb + hh + 1)).astype(F32)
        b.append(jnp.exp2(hv * (-8.0 / n_heads)) * t_q)

    def scores(n, hh):
        start = pl.multiple_of(n * tq, tq)
        return jnp.dot(ka_ref[0, hh, pl.ds(start, tq), :], qat_scr[hh],
                       preferred_element_type=F32)

    def key_max(s):
        return jnp.max(s, axis=0, keepdims=True)

    odd = i % 2
    near = jnp.maximum(i - 1, 0)
    near_bias = jnp.where(odd == 1, 0.0, NEG_INF)
    causal = (lax.broadcasted_iota(jnp.int32, (tq, tq), 1)
              >= lax.broadcasted_iota(jnp.int32, (tq, tq), 0))
    s0 = [jnp.where(causal, scores(i, hh), NEG_INF) for hh in heads]
    s1 = [scores(near, hh) + near_bias for hh in heads]
    m0 = [jnp.maximum(key_max(s0[hh]), key_max(s1[hh])) - b[hh] for hh in heads]
    p0 = [jnp.exp(s0[hh] - (m0[hh] + b[hh])).astype(BF16) for hh in heads]
    p1 = [jnp.exp(s1[hh] - (m0[hh] + b[hh])).astype(BF16) for hh in heads]
    for hh in heads:
        acc_scr[hh] = (jnp.dot(vt_ref[0, hh, i], p0[hh], preferred_element_type=F32)
                       + jnp.dot(vt_ref[0, hh, near], p1[hh], preferred_element_type=F32))
        m_scr[hh] = m0[hh]

    rest = i - odd

    def pair(t, carry):
        na = rest - 1 - 2 * t
        nb = na - 1
        sa = [scores(na, hh) for hh in heads]
        sb = [scores(nb, hh) for hh in heads]
        m_prev = [m_scr[hh] for hh in heads]
        m_new = [jnp.maximum(m_prev[hh], jnp.maximum(key_max(sa[hh]), key_max(sb[hh])) - b[hh])
                 for hh in heads]
        alpha = [jnp.exp(m_prev[hh] - m_new[hh]) for hh in heads]
        pa = [jnp.exp(sa[hh] - (m_new[hh] + b[hh])).astype(BF16) for hh in heads]
        pb = [jnp.exp(sb[hh] - (m_new[hh] + b[hh])).astype(BF16) for hh in heads]
        for hh in heads:
            acc_scr[hh] = (alpha[hh] * acc_scr[hh]
                           + jnp.dot(vt_ref[0, hh, na], pa[hh], preferred_element_type=F32)
                           + jnp.dot(vt_ref[0, hh, nb], pb[hh], preferred_element_type=F32))
            m_scr[hh] = m_new[hh]
        return carry

    lax.fori_loop(0, rest // 2, pair, 0)
    for hh in heads:
        acc = acc_scr[hh]
        ot = acc[0:HEAD_DIM, :] / acc[HEAD_DIM:HEAD_DIM + 1, :]
        o_ref[:, hh * HEAD_DIM:(hh + 1) * HEAD_DIM] = ot.T.astype(o_ref.dtype)


def _moba(qt, km, ka, vt, *, seq):
    batch, n_heads, _, _ = qt.shape
    d = n_heads * HEAD_DIM
    nb = seq // MOBA_BLOCK
    nbp = km.shape[2]
    assert nbp == _padded_blocks(seq) and nbp + ALIBI_TERMS <= LANES
    hb = MOBA_HEADS if n_heads % MOBA_HEADS == 0 else 1
    tq = MOBA_BLOCK
    wide = HEAD_DIM + LANES
    kern = functools.partial(_moba_kernel, n_heads=n_heads, hb=hb)
    return pl.pallas_call(
        kern,
        grid=(batch, n_heads // hb, nb),
        in_specs=[
            pl.BlockSpec((1, hb, HEAD_DIM, tq), lambda b, g, i: (b, g, 0, i)),
            pl.BlockSpec((1, hb, nbp, HEAD_DIM), lambda b, g, i: (b, g, 0, 0)),
            pl.BlockSpec((1, hb, seq, wide), lambda b, g, i: (b, g, 0, 0)),
            pl.BlockSpec((1, hb, nb, HEAD_DIM + VT_PAD, tq), lambda b, g, i: (b, g, 0, 0, 0),
                         pipeline_mode=pl.Buffered(1)),
        ],
        out_specs=pl.BlockSpec((tq, hb * HEAD_DIM), lambda b, g, i: (b * nb + i, g)),
        out_shape=jax.ShapeDtypeStruct((batch * seq, d), BF16),
        scratch_shapes=[
            pltpu.VMEM((hb, wide, tq), BF16),
            pltpu.VMEM((hb, HEAD_DIM + VT_PAD, tq), F32),
            pltpu.VMEM((hb, 1, tq), F32),
        ],
        compiler_params=_params(3),
        name="moba",
    )(qt, km, ka, vt)


def _wo_residual_kernel(o_ref, w_ref, x_ref, out_ref):
    out_ref[...] = x_ref[...] + jnp.dot(o_ref[...], w_ref[...], preferred_element_type=F32)


def _wo_residual(o, w, x, *, tm, tn):
    m, d = x.shape
    return pl.pallas_call(
        _wo_residual_kernel,
        grid=(m // tm, d // tn),
        in_specs=[
            pl.BlockSpec((tm, d), lambda i, j: (i, 0)),
            pl.BlockSpec((d, tn), lambda i, j: (0, j)),
            pl.BlockSpec((tm, tn), lambda i, j: (i, j)),
        ],
        out_specs=pl.BlockSpec((tm, tn), lambda i, j: (i, j)),
        out_shape=jax.ShapeDtypeStruct((m, d), F32),
        compiler_params=_params(2),
        name="wo_residual",
    )(o, w, x)


def _tiles(seq, d, f):
    def pick(n, cands):
        for c in cands:
            if n % c == 0:
                return c
        raise ValueError(f"no tile for {n}")
    return dict(
        tm=pick(seq, (512, 256)),
        tm_conv=pick(seq, (256,)),
        tn=pick(d, (512, 256, 128)),
        tf=pick(f, (512, 256, 128)),
    )


def kernel(x, conv_norm_g, conv_w_pw1, conv_b_pw1, conv_w_dw, conv_b_dw, conv_ln_g, conv_ln_b,
           conv_w_pw2, conv_b_pw2, kv_norm_g, w_kv, k_norm_g, attn_norm_g, w_q, q_norm_g, w_o,
           ffn_norm_g, ffn_w_in, ffn_w_conv, ffn_w_out):
    batch, seq, d = x.shape
    depth = ffn_w_in.shape[0]
    n_a = conv_w_pw1.shape[0]
    f = ffn_w_out.shape[1]
    t = _tiles(seq, d, f)
    n_heads = d // HEAD_DIM
    nb = seq // MOBA_BLOCK
    row = lambda v: v.reshape(1, -1)

    xf = x.reshape(batch * seq, d)
    kv = None
    for layer in range(depth):
        if layer < n_a:
            a = layer
            u = _pw1_glu(xf, row(conv_norm_g[a]), conv_w_pw1[a].astype(BF16), row(conv_b_pw1[a]),
                         tm=t["tm"], tn=t["tn"])
            xf = _conv_ln_pw2(u, xf, conv_w_dw[a], row(conv_b_dw[a]), row(conv_ln_g[a]),
                              row(conv_ln_b[a]), conv_w_pw2[a].astype(BF16), row(conv_b_pw2[a]),
                              tm=t["tm_conv"], seq=seq)
        else:
            b = layer - n_a
            qt, ka, vt, km = _qkv(xf, row(kv_norm_g), row(attn_norm_g[b]), w_kv.astype(BF16),
                                  w_q[b].astype(BF16), row(k_norm_g), row(q_norm_g[b]),
                                  tm=t["tm"], tn=t["tn"], batch=batch, seq=seq)
            if kv is None:
                km = km.transpose(0, 2, 1, 3, 4).reshape(batch, n_heads, nb, HEAD_DIM)
                km = jnp.pad(km, ((0, 0), (0, 0), (0, -nb % BF16_SUBLANES), (0, 0)))
                kv = (ka, vt, km)
            ka, vt, km = kv
            o = _moba(qt, km, ka, vt, seq=seq)
            xf = _wo_residual(o, w_o[b].astype(BF16), xf, tm=t["tm"], tn=d)
        xf = _conv_ffn(xf, row(ffn_norm_g[layer]), ffn_w_in[layer].astype(BF16), ffn_w_conv[layer],
                       ffn_w_out[layer].astype(BF16), tm=t["tm"], tf=t["tf"], tn=t["tn"], seq=seq)
    return xf.reshape(batch, seq, d)
```
